```python
import jax, jax.numpy as jnp
from jax import lax
import numpy as np

D_MODEL = 1024
BATCH = 8
SEQ = 4096
DEPTH = 1

N_META = 16
ATTN_HEADS = 8
HEAD_DIM = D_MODEL // 16
D_ATTN = ATTN_HEADS * HEAD_DIM
CONV_GROUPS = 8
D_CONV = D_MODEL // 2
CONV_WIDTH = 3
D_MIX = D_ATTN + D_CONV
Q_BLOCK = 128
EPS = 1e-6
SPLIT_SIZES = (D_ATTN, D_ATTN, D_ATTN, ATTN_HEADS, D_ATTN, D_CONV, D_CONV, D_CONV, D_CONV)
D_IN = sum(SPLIT_SIZES)
SPLIT_POINTS = tuple(int(s) for s in np.cumsum(SPLIT_SIZES)[:-1])

kernel_name = "hymba_fox_shortconv_hybrid"


def _rmsnorm(x, g):
    xf = x.astype(jnp.float32)
    y = xf * lax.rsqrt(jnp.mean(xf * xf, axis=-1, keepdims=True) + EPS)
    return (y * g.astype(jnp.float32)).astype(x.dtype)


def _group_rmsnorm(y, g, n_groups):
    lead = y.shape[:-1]
    c = y.shape[-1]
    yf = y.astype(jnp.float32).reshape(lead + (n_groups, c // n_groups))
    yf = yf * lax.rsqrt(jnp.mean(yf * yf, axis=-1, keepdims=True) + EPS)
    return (yf.reshape(lead + (c,)) * g.astype(jnp.float32)).astype(y.dtype)


def _fox_attention(q, k, v, cum_logf):
    b, l, h, dh = q.shape
    scale = dh ** -0.5
    key_pos = jnp.arange(l)
    c_keys = jnp.transpose(cum_logf, (0, 2, 1))

    def block(args):
        q_blk, c_blk, t_blk = args
        s = jnp.einsum('bqhd,bkhd->bhqk', q_blk, k,
                       preferred_element_type=jnp.float32) * scale
        s = s + jnp.transpose(c_blk, (0, 2, 1))[..., :, None] - c_keys[:, :, None, :]
        s = jnp.where(key_pos[None, :] <= t_blk[:, None], s, -jnp.inf)
        p = jax.nn.softmax(s, axis=-1)
        return jnp.einsum('bhqk,bkhd->bqhd', p.astype(v.dtype), v)

    out_meta = block((q[:, :N_META], cum_logf[:, :N_META], key_pos[:N_META]))
    n_blk = (l - N_META) // Q_BLOCK
    q_r = jnp.transpose(q[:, N_META:].reshape(b, n_blk, Q_BLOCK, h, dh), (1, 0, 2, 3, 4))
    c_r = jnp.transpose(cum_logf[:, N_META:].reshape(b, n_blk, Q_BLOCK, h), (1, 0, 2, 3))
    t_r = key_pos[N_META:].reshape(n_blk, Q_BLOCK)
    out_r = lax.map(block, (q_r, c_r, t_r))
    out_r = jnp.transpose(out_r, (1, 0, 2, 3, 4)).reshape(b, l - N_META, h, dh)
    return jnp.concatenate([out_meta, out_r], axis=1)


def _causal_depthwise_conv(x, w):
    c = x.shape[-1]
    return lax.conv_general_dilated(
        x, w.reshape(CONV_WIDTH, 1, c).astype(x.dtype),
        window_strides=(1,), padding=[(CONV_WIDTH - 1, 0)],
        dimension_numbers=('NWC', 'WIO', 'NWC'), feature_group_count=c)


def _hybrid_layer(h, norm_g, w_in, b_f, conv_w, attn_norm_g, conv_norm_g, w_out):
    b, l, _ = h.shape
    u = _rmsnorm(h, norm_g)
    proj = jnp.einsum('bld,de->ble', u, w_in)
    q, k, v, f_logit, z_attn, gate_b, gate_c, xc, z_conv = jnp.split(proj, SPLIT_POINTS, axis=-1)

    log_f = jax.nn.log_sigmoid(f_logit.astype(jnp.float32) + b_f.astype(jnp.float32))
    cum_logf = jnp.cumsum(log_f, axis=1)
    shp = (b, l, ATTN_HEADS, HEAD_DIM)
    attn = _fox_attention(q.reshape(shp), k.reshape(shp), v.reshape(shp), cum_logf)
    y_attn = _group_rmsnorm(attn.reshape(b, l, D_ATTN), attn_norm_g, ATTN_HEADS) * jax.nn.silu(z_attn)

    conv = _causal_depthwise_conv(gate_c * xc, conv_w)
    y_conv = _group_rmsnorm(gate_b * conv, conv_norm_g, CONV_GROUPS) * jax.nn.silu(z_conv)

    mix = jnp.concatenate([y_attn, y_conv], axis=-1)
    return h + jnp.einsum('ble,ed->bld', mix, w_out)


def setup_inputs(seed: int = 0) -> dict:
    key = jax.random.key(seed)
    ks = jax.random.split(key, 10)
    f32 = jnp.float32
    x = jax.random.normal(ks[0], (BATCH, SEQ, D_MODEL), f32)
    meta = jax.random.normal(ks[1], (N_META, D_MODEL), f32)
    norm_g = 1.0 + 0.02 * jax.random.normal(ks[2], (DEPTH, D_MODEL), f32)
    w_in = jax.random.normal(ks[3], (DEPTH, D_MODEL, D_IN), f32) * D_MODEL ** -0.5
    b_f = jax.random.uniform(ks[4], (DEPTH, ATTN_HEADS), f32, minval=1.0, maxval=5.0)
    conv_w = jax.random.normal(ks[5], (DEPTH, CONV_WIDTH, D_CONV), f32) * CONV_WIDTH ** -0.5
    attn_norm_g = 1.0 + 0.02 * jax.random.normal(ks[6], (DEPTH, D_ATTN), f32)
    conv_norm_g = 1.0 + 0.02 * jax.random.normal(ks[7], (DEPTH, D_CONV), f32)
    w_out = jax.random.normal(ks[8], (DEPTH, D_MIX, D_MODEL), f32) * D_MIX ** -0.5
    final_norm_g = 1.0 + 0.02 * jax.random.normal(ks[9], (D_MODEL,), f32)
    return {"x": x, "meta": meta, "norm_g": norm_g, "w_in": w_in, "b_f": b_f,
            "conv_w": conv_w, "attn_norm_g": attn_norm_g, "conv_norm_g": conv_norm_g,
            "w_out": w_out, "final_norm_g": final_norm_g}


def reference(x, meta, norm_g, w_in, b_f, conv_w, attn_norm_g, conv_norm_g, w_out, final_norm_g):
    b = x.shape[0]
    meta_b = jnp.broadcast_to(meta.astype(x.dtype)[None], (b, N_META, x.shape[-1]))
    h = jnp.concatenate([meta_b, x], axis=1)
    for layer in range(DEPTH):
        h = _hybrid_layer(h, norm_g[layer], w_in[layer], b_f[layer], conv_w[layer],
                          attn_norm_g[layer], conv_norm_g[layer], w_out[layer])
    return _rmsnorm(h[:, N_META:], final_norm_g)
```

```python
import functools

import jax
import jax.numpy as jnp
from jax import lax
from jax.experimental import pallas as pl
from jax.experimental.pallas import tpu as pltpu

D_MODEL = 1024
N_META = 16
ATTN_HEADS = 8
HEAD_DIM = 64
D_ATTN = 512
D_CONV = 512
CONV_WIDTH = 3
EPS = 1e-6
LANES = 128
SUBLANES = 8
F_REP = 3
VMEM_LIMIT = 48 * 1024 * 1024

_BF16 = jnp.bfloat16
_F32 = jnp.float32


def _log_sigmoid(x):
    return jnp.minimum(x, 0.0) - jnp.log1p(jnp.exp(-jnp.abs(x)))


def _silu(x):
    return x * (1.0 / (1.0 + jnp.exp(-x)))


def _dot_nt(a, b):
    return lax.dot_general(a, b, (((1,), (1,)), ((), ())),
                           preferred_element_type=_F32)


def _inproj_kernel(x_ref, g_ref, w_ref, wf_ref, bf_ref, cw_ref, cg_ref, halo0_ref,
                   q_ref, k_ref, v_ref, z_ref, yc_ref, ct_ref, *rest,
                   tm, emit_cx):
    if emit_cx:
        cx_ref, halo_s, carry_s, cx_s = rest
    else:
        halo_s, carry_s, cx_s = rest
    i = pl.program_id(1)

    @pl.when(i == 0)
    def _():
        halo_s[...] = halo0_ref[...]
        carry_s[...] = jnp.zeros_like(carry_s)

    x = x_ref[0]
    ms = jnp.mean(x * x, axis=-1, keepdims=True)
    u = (x * lax.rsqrt(ms + EPS) * g_ref[...]).astype(_BF16)

    def proj(c):
        return jnp.dot(u, w_ref[:, c * 512:(c + 1) * 512],
                       preferred_element_type=_F32)

    q_ref[0] = (proj(0) * (HEAD_DIM ** -0.5)).astype(_BF16)
    k_ref[0] = proj(1).astype(_BF16)
    v_ref[0] = proj(2).astype(_BF16)
    z_ref[0] = proj(3).astype(_BF16)

    fl = jnp.dot(u, wf_ref[...], preferred_element_type=_F32)
    logf = _log_sigmoid(fl + bf_ref[...])
    hi = logf.astype(_BF16).astype(_F32)
    r1 = logf - hi
    mid = r1.astype(_BF16).astype(_F32)
    lo = r1 - mid
    grp = lax.broadcasted_iota(jnp.int32, (1, LANES), 1) // ATTN_HEADS
    pieces = jnp.where(grp == 0, hi, jnp.where(grp == 1, mid,
                       jnp.where(grp == 2, lo, 0.0))).astype(_BF16)
    row = lax.broadcasted_iota(jnp.int32, (tm, tm), 0)
    col = lax.broadcasted_iota(jnp.int32, (tm, tm), 1)
    tri = jnp.where(col <= row, 1.0, 0.0).astype(_BF16)
    c3 = jnp.dot(tri, pieces, preferred_element_type=_F32)
    csum = (c3 + pltpu.roll(c3, LANES - ATTN_HEADS, axis=1)
            + pltpu.roll(c3, LANES - 2 * ATTN_HEADS, axis=1))
    csum = csum + carry_s[...]
    carry_s[...] = csum[tm - 1:tm, :]
    ct_ref[0] = csum.T[0:ATTN_HEADS, :]

    gate_b = proj(4)
    cx = proj(5) * proj(6)
    cx_s[0:SUBLANES, :] = halo_s[...]
    cx_s[SUBLANES:SUBLANES + tm, :] = cx
    if emit_cx:
        cx_ref[0] = cx
    conv = (cw_ref[0:1, :] * cx_s[SUBLANES - 2:SUBLANES - 2 + tm, :]
            + cw_ref[1:2, :] * cx_s[SUBLANES - 1:SUBLANES - 1 + tm, :]
            + cw_ref[2:3, :] * cx)
    halo_s[...] = cx[tm - SUBLANES:tm, :]
    yb = gate_b * conv
    zc = proj(7)
    low = lax.broadcasted_iota(jnp.int32, (1, LANES), 1) < HEAD_DIM
    for cb in range(D_CONV // LANES):
        sl = slice(cb * LANES, (cb + 1) * LANES)
        y = yb[:, sl]
        y2 = y * y
        s_lo = jnp.sum(jnp.where(low, y2, 0.0), axis=-1, keepdims=True)
        s_hi = jnp.sum(jnp.where(low, 0.0, y2), axis=-1, keepdims=True)
        msq = jnp.where(low, s_lo, s_hi) * (1.0 / HEAD_DIM)
        yn = y * lax.rsqrt(msq + EPS) * cg_ref[:, sl]
        yc_ref[0, :, sl] = (yn * _silu(zc[:, sl])).astype(_BF16)


def _inproj(x3, g, w_main, wf, bf, cw, cg, halo0, *, tm, emit_cx):
    b, l, d = x3.shape
    nt = l // tm
    kern = functools.partial(_inproj_kernel, tm=tm, emit_cx=emit_cx)
    const = lambda bi, i: (0, 0)
    row_blk = lambda bi, i: (bi, i, 0)
    out_shape = [jax.ShapeDtypeStruct((b, l, 512), _BF16)] * 5
    out_specs = [pl.BlockSpec((1, tm, 512), row_blk)] * 5
    out_shape.append(jax.ShapeDtypeStruct((b, ATTN_HEADS, l), _F32))
    out_specs.append(pl.BlockSpec((1, ATTN_HEADS, tm), lambda bi, i: (bi, 0, i)))
    if emit_cx:
        out_shape.append(jax.ShapeDtypeStruct((b, l, D_CONV), _F32))
        out_specs.append(pl.BlockSpec((1, tm, D_CONV), row_blk))
    return pl.pallas_call(
        kern,
        grid=(b, nt),
        in_specs=[
            pl.BlockSpec((1, tm, d), row_blk),
            pl.BlockSpec((1, d), const),
            pl.BlockSpec(w_main.shape, const),
            pl.BlockSpec(wf.shape, const),
            pl.BlockSpec((1, LANES), const),
            pl.BlockSpec((CONV_WIDTH, D_CONV), const),
            pl.BlockSpec((1, D_CONV), const),
            pl.BlockSpec((SUBLANES, D_CONV), const),
        ],
        out_specs=out_specs,
        out_shape=out_shape,
        scratch_shapes=[
            pltpu.VMEM((SUBLANES, D_CONV), _F32),
            pltpu.VMEM((1, LANES), _F32),
            pltpu.VMEM((tm + SUBLANES, D_CONV), _F32),
        ],
        compiler_params=pltpu.CompilerParams(
            dimension_semantics=("arbitrary", "arbitrary"),
            vmem_limit_bytes=VMEM_LIMIT),
        name="inproj_meta" if emit_cx else "inproj",
    )(x3, g, w_main, wf, bf, cw, cg, halo0)


def _attn_kernel(q_ref, k_ref, v_ref, km_ref, vm_ref, ck_ref, cm_ref, z_ref, g_ref,
                 o_ref, *, tq, tk):
    pair = pl.program_id(1)
    qi = pl.program_id(2)
    q = q_ref[0]
    low = lax.broadcasted_iota(jnp.int32, (1, LANES), 1) < HEAD_DIM
    n_full = (qi * tq) // tk
    row_ids = qi * tq + lax.broadcasted_iota(jnp.int32, (tq, tk), 0)
    col_ids = lax.broadcasted_iota(jnp.int32, (tq, tk), 1)

    normed = []
    for e in range(2):
        keep = low if e == 0 else jnp.logical_not(low)
        qe = jnp.where(keep, q, jnp.zeros_like(q))
        h = 2 * pair + e

        cm = cm_ref[pl.ds(h, 1), :]
        bias_m = cm_ref[pl.ds(h, 1), N_META - 1:N_META] - cm
        s = _dot_nt(qe, km_ref[...]) + bias_m
        m = jnp.max(s, axis=-1, keepdims=True)
        p = jnp.exp(s - m)
        l = jnp.sum(p, axis=-1, keepdims=True)
        acc = jnp.dot(p.astype(_BF16), vm_ref[...], preferred_element_type=_F32)

        def step(j, carry, masked):
            m, l, acc = carry
            start = pl.multiple_of(j * tk, tk)
            kb = k_ref[0, pl.ds(start, tk), :]
            vb = v_ref[0, pl.ds(start, tk), :]
            ck = ck_ref[0, pl.ds(h, 1), pl.ds(start, tk)]
            s = _dot_nt(qe, kb) - ck
            if masked:
                s = jnp.where(col_ids + j * tk <= row_ids, s, -jnp.inf)
            m_new = jnp.maximum(m, jnp.max(s, axis=-1, keepdims=True))
            alpha = jnp.exp(m - m_new)
            p = jnp.exp(s - m_new)
            l_new = alpha * l + jnp.sum(p, axis=-1, keepdims=True)
            acc_new = alpha * acc + jnp.dot(p.astype(_BF16), vb,
                                            preferred_element_type=_F32)
            return m_new, l_new, acc_new

        carry = lax.fori_loop(0, n_full,
                              functools.partial(step, masked=False), (m, l, acc))
        for d in range(tq // tk):
            carry = step(n_full + d, carry, masked=True)
        m, l, acc = carry
        o = acc / l
        msq = jnp.sum(jnp.where(keep, o * o, 0.0), axis=-1, keepdims=True)
        normed.append(o * lax.rsqrt(msq * (1.0 / HEAD_DIM) + EPS))

    y = jnp.where(low, normed[0], normed[1]) * g_ref[...]
    z = z_ref[0].astype(_F32)
    o_ref[0] = (y * _silu(z)).astype(_BF16)


def _attention(q, k, v, km, vm, ck, cm, z, g, *, tq, tk):
    b, l, _ = q.shape
    n_pair = D_ATTN // LANES
    kern = functools.partial(_attn_kernel, tq=tq, tk=tk)
    return pl.pallas_call(
        kern,
        grid=(b, n_pair, l // tq),
        in_specs=[
            pl.BlockSpec((1, tq, LANES), lambda bi, p, i: (bi, i, p)),
            pl.BlockSpec((1, l, LANES), lambda bi, p, i: (bi, 0, p)),
            pl.BlockSpec((1, l, LANES), lambda bi, p, i: (bi, 0, p)),
            pl.BlockSpec((N_META, LANES), lambda bi, p, i: (0, p)),
            pl.BlockSpec((N_META, LANES), lambda bi, p, i: (0, p)),
            pl.BlockSpec((1, ATTN_HEADS, l), lambda bi, p, i: (bi, 0, 0)),
            pl.BlockSpec((ATTN_HEADS, N_META), lambda bi, p, i: (0, 0)),
            pl.BlockSpec((1, tq, LANES), lambda bi, p, i: (bi, i, p)),
            pl.BlockSpec((1, LANES), lambda bi, p, i: (0, p)),
        ],
        out_specs=pl.BlockSpec((1, tq, LANES), lambda bi, p, i: (bi, i, p)),
        out_shape=jax.ShapeDtypeStruct((b, l, D_ATTN), _BF16),
        compiler_params=pltpu.CompilerParams(
            dimension_semantics=("arbitrary", "arbitrary", "arbitrary"),
            vmem_limit_bytes=VMEM_LIMIT),
        name="fox_attention",
    )(q, k, v, km, vm, ck, cm, z, g)


def _outproj_kernel(x_ref, ya_ref, yc_ref, w_ref, g_ref, o_ref):
    hres = (x_ref[0]
            + jnp.dot(ya_ref[0], w_ref[0:D_ATTN, :], preferred_element_type=_F32)
            + jnp.dot(yc_ref[0], w_ref[D_ATTN:, :], preferred_element_type=_F32))
    ms = jnp.mean(hres * hres, axis=-1, keepdims=True)
    o_ref[0] = hres * lax.rsqrt(ms + EPS) * g_ref[...]


def _outproj(x, ya, yc, w_out, g, *, tm):
    b, l, d = x.shape
    row_blk = lambda bi, i: (bi, i, 0)
    const = lambda bi, i: (0, 0)
    return pl.pallas_call(
        _outproj_kernel,
        grid=(b, l // tm),
        in_specs=[
            pl.BlockSpec((1, tm, d), row_blk),
            pl.BlockSpec((1, tm, D_ATTN), row_blk),
            pl.BlockSpec((1, tm, D_CONV), row_blk),
            pl.BlockSpec(w_out.shape, const),
            pl.BlockSpec((1, d), const),
        ],
        out_specs=pl.BlockSpec((1, tm, d), row_blk),
        out_shape=jax.ShapeDtypeStruct((b, l, d), _F32),
        compiler_params=pltpu.CompilerParams(
            dimension_semantics=("arbitrary", "arbitrary"),
            vmem_limit_bytes=VMEM_LIMIT),
        name="outproj",
    )(x, ya, yc, w_out, g)


def kernel(x, meta, norm_g, w_in, b_f, conv_w, attn_norm_g, conv_norm_g, w_out, final_norm_g):
    assert norm_g.shape[0] == 1, "single layer only"
    b, seq, d = x.shape
    w = w_in[0]
    f0 = 3 * D_ATTN
    w_main = jnp.concatenate([w[:, :f0], w[:, f0 + ATTN_HEADS:]], axis=1).astype(_BF16)
    wf8 = w[:, f0:f0 + ATTN_HEADS]
    pad = jnp.zeros((d, LANES - F_REP * ATTN_HEADS), w.dtype)
    wf = jnp.concatenate([wf8] * F_REP + [pad], axis=1).astype(_BF16)
    bf = jnp.concatenate([b_f[0]] * F_REP
                         + [jnp.zeros((LANES - F_REP * ATTN_HEADS,), b_f.dtype)])[None, :]
    g_in = norm_g
    cw = conv_w[0]
    cg = conv_norm_g

    meta_rows = LANES
    meta_p = jnp.pad(meta.astype(x.dtype), ((0, meta_rows - N_META), (0, 0)))[None]
    zero_halo = jnp.zeros((SUBLANES, D_CONV), _F32)
    _, km, vm, _, _, ctm, cxm = _inproj(meta_p, g_in, w_main, wf, bf, cw, cg, zero_halo,
                                        tm=meta_rows, emit_cx=True)
    km = km[0, :N_META]
    vm = vm[0, :N_META]
    cm = ctm[0, :, :N_META]
    halo0 = cxm[0, N_META - SUBLANES:N_META]

    q, k, v, z, yc, ck = _inproj(x, g_in, w_main, wf, bf, cw, cg, halo0,
                                 tm=512, emit_cx=False)
    ya = _attention(q, k, v, km, vm, ck, cm, z, attn_norm_g, tq=512, tk=512)
    return _outproj(x, ya, yc, w_out[0].astype(_BF16), final_norm_g[None, :], tm=512)
```

```python
import functools
import math

import jax
import jax.numpy as jnp
from jax import lax
from jax.experimental import pallas as pl
from jax.experimental.pallas import tpu as pltpu

D_MODEL = 1024
N_META = 16
ATTN_HEADS = 8
HEAD_DIM = 64
D_ATTN = 512
D_CONV = 512
CONV_WIDTH = 3
EPS = 1e-6
LANES = 128
SUBLANES = 8
N_PIECES = 3
LOG2E = math.log2(math.e)
VMEM_LIMIT = 48 * 1024 * 1024

_BF16 = jnp.bfloat16
_F32 = jnp.float32


def _log_sigmoid(x):
    return jnp.minimum(x, 0.0) - jnp.log1p(jnp.exp(-jnp.abs(x)))


def _silu(x):
    return x * (1.0 / (1.0 + jnp.exp(-x)))


def _dot_nt(a, b):
    return lax.dot_general(a, b, (((1,), (1,)), ((), ())),
                           preferred_element_type=_F32)


def _bf16_pieces(x, grp):
    hi = x.astype(_BF16).astype(_F32)
    r1 = x - hi
    mid = r1.astype(_BF16).astype(_F32)
    lo = r1 - mid
    return jnp.where(grp == 0, hi, jnp.where(grp == 1, mid,
                     jnp.where(grp == 2, lo, 0.0))).astype(_BF16)


def _inproj_kernel(x_ref, g_ref, w_ref, wqv_ref, wf_ref, bf_ref, cw_ref, cg_ref, halo0_ref,
                   carry0_ref, qt_ref, k_ref, vt_ref, z_ref, yc_ref, aug_ref, *rest,
                   tm, emit_tail):
    if emit_tail:
        cx_ref, crow_ref, halo_s, carry_s, cx_s = rest
    else:
        halo_s, carry_s, cx_s = rest
    i = pl.program_id(1)

    @pl.when(i == 0)
    def _():
        halo_s[...] = halo0_ref[...]
        carry_s[...] = carry0_ref[...]

    x = x_ref[0]
    ms = jnp.mean(x * x, axis=-1, keepdims=True)
    u = (x * lax.rsqrt(ms + EPS) * g_ref[...]).astype(_BF16)

    def proj(c):
        return jnp.dot(u, w_ref[:, c * 512:(c + 1) * 512],
                       preferred_element_type=_F32)

    qvt = _dot_nt(wqv_ref[...], u)
    qt_ref[0] = (qvt[:D_ATTN] * (LOG2E * HEAD_DIM ** -0.5)).astype(_BF16)
    vt_ref[0] = qvt[D_ATTN:].astype(_BF16)
    k_ref[0] = proj(0).astype(_BF16)
    z_ref[0] = proj(1).astype(_BF16)

    fl = jnp.dot(u, wf_ref[...], preferred_element_type=_F32)
    logf = _log_sigmoid(fl + bf_ref[...])
    lane = lax.broadcasted_iota(jnp.int32, (1, LANES), 1)
    grp = lane // ATTN_HEADS
    row = lax.broadcasted_iota(jnp.int32, (tm, tm), 0)
    col = lax.broadcasted_iota(jnp.int32, (tm, tm), 1)
    tri = jnp.where(col <= row, 1.0, 0.0).astype(_BF16)
    c3 = jnp.dot(tri, _bf16_pieces(logf, grp), preferred_element_type=_F32)
    h8 = ATTN_HEADS
    csum = (c3 + pltpu.roll(c3, h8, axis=1) + pltpu.roll(c3, 2 * h8, axis=1)
            + pltpu.roll(c3, LANES - h8, axis=1) + pltpu.roll(c3, LANES - 2 * h8, axis=1))
    csum = csum + carry_s[...]
    carry_s[...] = csum[tm - 1:tm, :]
    aug_ref[0] = _bf16_pieces(csum * (-LOG2E), grp)
    if emit_tail:
        crow_ref[0] = csum

    gate_b = proj(2)
    cx = proj(3) * proj(4)
    cx_s[0:SUBLANES, :] = halo_s[...]
    cx_s[SUBLANES:SUBLANES + tm, :] = cx
    if emit_tail:
        cx_ref[0] = cx
    conv = (cw_ref[0:1, :] * cx_s[SUBLANES - 2:SUBLANES - 2 + tm, :]
            + cw_ref[1:2, :] * cx_s[SUBLANES - 1:SUBLANES - 1 + tm, :]
            + cw_ref[2:3, :] * cx)
    halo_s[...] = cx[tm - SUBLANES:tm, :]
    yb = gate_b * conv
    zc = proj(5)
    low = lane < HEAD_DIM
    for cb in range(D_CONV // LANES):
        sl = slice(cb * LANES, (cb + 1) * LANES)
        y = yb[:, sl]
        y2 = y * y
        s_lo = jnp.sum(jnp.where(low, y2, 0.0), axis=-1, keepdims=True)
        s_hi = jnp.sum(jnp.where(low, 0.0, y2), axis=-1, keepdims=True)
        msq = jnp.where(low, s_lo, s_hi) * (1.0 / HEAD_DIM)
        yn = y * lax.rsqrt(msq + EPS) * cg_ref[:, sl]
        yc_ref[0, :, sl] = (yn * _silu(zc[:, sl])).astype(_BF16)


def _inproj(x3, g, w_main, wqv_t, wf, bf, cw, cg, halo0, carry0, *, tm, emit_tail):
    b, l, d = x3.shape
    nt = l // tm
    kern = functools.partial(_inproj_kernel, tm=tm, emit_tail=emit_tail)
    const = lambda bi, i: (0, 0)
    row_blk = lambda bi, i: (bi, i, 0)
    col_blk = lambda bi, i: (bi, 0, i)
    rows = (jax.ShapeDtypeStruct((b, l, 512), _BF16), pl.BlockSpec((1, tm, 512), row_blk))
    cols = (jax.ShapeDtypeStruct((b, 512, l), _BF16), pl.BlockSpec((1, 512, tm), col_blk))
    out_shape, out_specs = (list(t) for t in zip(cols, rows, cols, rows, rows))
    out_shape.append(jax.ShapeDtypeStruct((b, l, LANES), _BF16))
    out_specs.append(pl.BlockSpec((1, tm, LANES), row_blk))
    if emit_tail:
        out_shape.append(jax.ShapeDtypeStruct((b, l, D_CONV), _F32))
        out_specs.append(pl.BlockSpec((1, tm, D_CONV), row_blk))
        out_shape.append(jax.ShapeDtypeStruct((b, l, LANES), _F32))
        out_specs.append(pl.BlockSpec((1, tm, LANES), row_blk))
    return pl.pallas_call(
        kern,
        grid=(b, nt),
        in_specs=[
            pl.BlockSpec((1, tm, d), row_blk),
            pl.BlockSpec((1, d), const),
            pl.BlockSpec(w_main.shape, const),
            pl.BlockSpec(wqv_t.shape, const),
            pl.BlockSpec(wf.shape, const),
            pl.BlockSpec((1, LANES), const),
            pl.BlockSpec((CONV_WIDTH, D_CONV), const),
            pl.BlockSpec((1, D_CONV), const),
            pl.BlockSpec((SUBLANES, D_CONV), const),
            pl.BlockSpec((1, LANES), const),
        ],
        out_specs=out_specs,
        out_shape=out_shape,
        scratch_shapes=[
            pltpu.VMEM((SUBLANES, D_CONV), _F32),
            pltpu.VMEM((1, LANES), _F32),
            pltpu.VMEM((tm + SUBLANES, D_CONV), _F32),
        ],
        compiler_params=pltpu.CompilerParams(
            dimension_semantics=("arbitrary", "arbitrary"),
            vmem_limit_bytes=VMEM_LIMIT),
        name="inproj_meta" if emit_tail else "inproj",
    )(x3, g, w_main, wqv_t, wf, bf, cw, cg, halo0, carry0)


def _attn_kernel(qt_ref, k_ref, aug_ref, vt_ref, km_ref, augm_ref, vmt_ref, z_ref, g_ref,
                 o_ref, *, tq, tk):
    pair = pl.program_id(1)
    qi = pl.program_id(2)
    qt = qt_ref[0]
    sub = lax.broadcasted_iota(jnp.int32, (LANES, 1), 0)
    top = sub < HEAD_DIM
    zero = jnp.zeros_like(qt)

    def ones_for(h):
        sel = jnp.logical_and(sub % ATTN_HEADS == h, sub < N_PIECES * ATTN_HEADS)
        return jnp.broadcast_to(jnp.where(sel, 1.0, 0.0).astype(_BF16), (LANES, tq))

    qa = jnp.concatenate([jnp.where(top, qt, zero), ones_for(2 * pair)], axis=0)
    qb = jnp.concatenate([jnp.where(top, zero, qt), ones_for(2 * pair + 1)], axis=0)
    q2t = jnp.concatenate([qa, qb], axis=1)

    n_full = (qi * tq) // tk
    c = lax.broadcasted_iota(jnp.int32, (tk, 2 * tq), 1)
    q_ids = qi * tq + jnp.where(c >= tq, c - tq, c)
    k_ids = lax.broadcasted_iota(jnp.int32, (tk, 2 * tq), 0)

    s = jnp.dot(jnp.concatenate([km_ref[...], augm_ref[...]], axis=1), q2t,
                preferred_element_type=_F32)
    m = jnp.max(s, axis=0, keepdims=True)
    p = jnp.exp2(s - m)
    l = jnp.sum(p, axis=0, keepdims=True)
    acc = jnp.dot(vmt_ref[...], p.astype(_BF16), preferred_element_type=_F32)

    def step(j, carry, masked):
        m, l, acc = carry
        start = pl.multiple_of(j * tk, tk)
        kx = jnp.concatenate([k_ref[0, pl.ds(start, tk), :],
                              aug_ref[0, pl.ds(start, tk), :]], axis=1)
        vt = vt_ref[0, :, pl.ds(start, tk)]
        s = jnp.dot(kx, q2t, preferred_element_type=_F32)
        if masked:
            s = jnp.where(k_ids + j * tk <= q_ids, s, -jnp.inf)
        m_new = jnp.maximum(m, jnp.max(s, axis=0, keepdims=True))
        alpha = jnp.exp2(m - m_new)
        p = jnp.exp2(s - m_new)
        l_new = alpha * l + jnp.sum(p, axis=0, keepdims=True)
        acc_new = alpha * acc + jnp.dot(vt, p.astype(_BF16), preferred_element_type=_F32)
        return m_new, l_new, acc_new

    carry = lax.fori_loop(0, n_full, functools.partial(step, masked=False), (m, l, acc))
    for d in range(tq // tk):
        carry = step(n_full + d, carry, masked=True)
    m, l, acc = carry

    ot = acc * (1.0 / l)
    oc = jnp.where(top, ot[:, :tq], ot[:, tq:])
    o2 = oc * oc
    msa = jnp.sum(jnp.where(top, o2, 0.0), axis=0, keepdims=True)
    msb = jnp.sum(jnp.where(top, 0.0, o2), axis=0, keepdims=True)
    inv = jnp.where(top, lax.rsqrt(msa * (1.0 / HEAD_DIM) + EPS),
                    lax.rsqrt(msb * (1.0 / HEAD_DIM) + EPS))
    y = (oc * inv).T * g_ref[...]
    z = z_ref[0].astype(_F32)
    o_ref[0] = (y * _silu(z)).astype(_BF16)


def _attention(qt, k, aug, vt, km, augm, vmt, z, g, *, tq, tk):
    b, l, _ = k.shape
    n_pair = D_ATTN // LANES
    kern = functools.partial(_attn_kernel, tq=tq, tk=tk)
    q_blk = lambda bi, p, i: (bi, i, p)
    return pl.pallas_call(
        kern,
        grid=(b, n_pair, l // tq),
        in_specs=[
            pl.BlockSpec((1, LANES, tq), lambda bi, p, i: (bi, p, i)),
            pl.BlockSpec((1, l, LANES), lambda bi, p, i: (bi, 0, p)),
            pl.BlockSpec((1, l, LANES), lambda bi, p, i: (bi, 0, 0)),
            pl.BlockSpec((1, LANES, l), lambda bi, p, i: (bi, p, 0)),
            pl.BlockSpec((N_META, LANES), lambda bi, p, i: (0, p)),
            pl.BlockSpec((N_META, LANES), lambda bi, p, i: (0, 0)),
            pl.BlockSpec((LANES, N_META), lambda bi, p, i: (p, 0)),
            pl.BlockSpec((1, tq, LANES), q_blk),
            pl.BlockSpec((1, LANES), lambda bi, p, i: (0, p)),
        ],
        out_specs=pl.BlockSpec((1, tq, LANES), q_blk),
        out_shape=jax.ShapeDtypeStruct((b, l, D_ATTN), _BF16),
        compiler_params=pltpu.CompilerParams(
            dimension_semantics=("arbitrary", "arbitrary", "arbitrary"),
            vmem_limit_bytes=VMEM_LIMIT),
        name="fox_attention",
    )(qt, k, aug, vt, km, augm, vmt, z, g)


def _outproj_kernel(x_ref, ya_ref, yc_ref, w_ref, g_ref, o_ref):
    hres = (x_ref[0]
            + jnp.dot(ya_ref[0], w_ref[0:D_ATTN, :], preferred_element_type=_F32)
            + jnp.dot(yc_ref[0], w_ref[D_ATTN:, :], preferred_element_type=_F32))
    ms = jnp.mean(hres * hres, axis=-1, keepdims=True)
    o_ref[0] = hres * lax.rsqrt(ms + EPS) * g_ref[...]


def _outproj(x, ya, yc, w_out, g, *, tm):
    b, l, d = x.shape
    row_blk = lambda bi, i: (bi, i, 0)
    const = lambda bi, i: (0, 0)
    return pl.pallas_call(
        _outproj_kernel,
        grid=(b, l // tm),
        in_specs=[
            pl.BlockSpec((1, tm, d), row_blk),
            pl.BlockSpec((1, tm, D_ATTN), row_blk),
            pl.BlockSpec((1, tm, D_CONV), row_blk),
            pl.BlockSpec(w_out.shape, const),
            pl.BlockSpec((1, d), const),
        ],
        out_specs=pl.BlockSpec((1, tm, d), row_blk),
        out_shape=jax.ShapeDtypeStruct((b, l, d), _F32),
        compiler_params=pltpu.CompilerParams(
            dimension_semantics=("arbitrary", "arbitrary"),
            vmem_limit_bytes=VMEM_LIMIT),
        name="outproj",
    )(x, ya, yc, w_out, g)


def kernel(x, meta, norm_g, w_in, b_f, conv_w, attn_norm_g, conv_norm_g, w_out, final_norm_g):
    assert norm_g.shape[0] == 1, "single layer only"
    b, seq, d = x.shape
    w = w_in[0]
    f0 = 3 * D_ATTN
    n_rep = N_PIECES * ATTN_HEADS
    w_main = jnp.concatenate([w[:, D_ATTN:2 * D_ATTN], w[:, f0 + ATTN_HEADS:]],
                             axis=1).astype(_BF16)
    wqv_t = jnp.concatenate([w[:, :D_ATTN], w[:, 2 * D_ATTN:f0]], axis=1).T.astype(_BF16)
    wf8 = w[:, f0:f0 + ATTN_HEADS]
    wf = jnp.concatenate([wf8] * N_PIECES + [jnp.zeros((d, LANES - n_rep), w.dtype)],
                         axis=1).astype(_BF16)
    bf = jnp.concatenate([b_f[0]] * N_PIECES
                         + [jnp.zeros((LANES - n_rep,), b_f.dtype)])[None, :]
    cw = conv_w[0]

    meta_rows = LANES
    meta_p = jnp.pad(meta.astype(x.dtype), ((0, meta_rows - N_META), (0, 0)))[None]
    zero_halo = jnp.zeros((SUBLANES, D_CONV), _F32)
    zero_carry = jnp.zeros((1, LANES), _F32)
    _, km, vmt, _, _, augm, cxm, crow = _inproj(
        meta_p, norm_g, w_main, wqv_t, wf, bf, cw, conv_norm_g, zero_halo, zero_carry,
        tm=meta_rows, emit_tail=True)
    km = km[0, :N_META]
    vmt = vmt[0, :, :N_META]
    augm = augm[0, :N_META]
    halo0 = cxm[0, N_META - SUBLANES:N_META]
    carry0 = crow[0, N_META - 1:N_META]

    qt, k, vt, z, yc, aug = _inproj(x, norm_g, w_main, wqv_t, wf, bf, cw, conv_norm_g,
                                    halo0, carry0, tm=512, emit_tail=False)
    ya = _attention(qt, k, aug, vt, km, augm, vmt, z, attn_norm_g, tq=512, tk=512)
    return _outproj(x, ya, yc, w_out[0].astype(_BF16), final_norm_g[None, :], tm=512)
```

```python
import functools
import math

import jax
import jax.numpy as jnp
from jax import lax
from jax.experimental import pallas as pl
from jax.experimental.pallas import tpu as pltpu

D_MODEL = 1024
N_META = 16
ATTN_HEADS = 8
HEAD_DIM = 64
D_ATTN = 512
D_CONV = 512
CONV_WIDTH = 3
EPS = 1e-6
LANES = 128
SUBLANES = 8
N_PIECES = 3
LOG2E = math.log2(math.e)
VMEM_LIMIT = 48 * 1024 * 1024

_BF16 = jnp.bfloat16
_F32 = jnp.float32


def _log_sigmoid(x):
    return jnp.minimum(x, 0.0) - jnp.log1p(jnp.exp(-jnp.abs(x)))


def _silu(x):
    return x * (1.0 / (1.0 + jnp.exp(-x)))


def _dot_nt(a, b):
    return lax.dot_general(a, b, (((1,), (1,)), ((), ())),
                           preferred_element_type=_F32)


def _bf16_pieces(x, grp):
    hi = x.astype(_BF16).astype(_F32)
    r1 = x - hi
    mid = r1.astype(_BF16).astype(_F32)
    lo = r1 - mid
    return jnp.where(grp == 0, hi, jnp.where(grp == 1, mid,
                     jnp.where(grp == 2, lo, 0.0))).astype(_BF16)


def _inproj_kernel(x_ref, g_ref, w_ref, wqv_ref, wf_ref, bf_ref, cw_ref, cg_ref, halo0_ref,
                   carry0_ref, qt_ref, k_ref, vt_ref, z_ref, yc_ref, aug_ref, *rest,
                   tm, emit_tail):
    if emit_tail:
        cx_ref, crow_ref, halo_s, carry_s, cx_s = rest
    else:
        halo_s, carry_s, cx_s = rest
    i = pl.program_id(1)

    @pl.when(i == 0)
    def _():
        halo_s[...] = halo0_ref[...]
        carry_s[...] = carry0_ref[...]

    x = x_ref[0]
    ms = jnp.mean(x * x, axis=-1, keepdims=True)
    u = (x * lax.rsqrt(ms + EPS) * g_ref[...]).astype(_BF16)

    def proj(c):
        return jnp.dot(u, w_ref[:, c * 512:(c + 1) * 512],
                       preferred_element_type=_F32)

    qvt = _dot_nt(wqv_ref[...], u)
    qt_ref[0] = (qvt[:D_ATTN] * (LOG2E * HEAD_DIM ** -0.5)).astype(_BF16)
    vt_ref[0] = qvt[D_ATTN:].astype(_BF16)
    k_ref[0] = proj(0).astype(_BF16)
    z_ref[0] = proj(1).astype(_BF16)

    fl = jnp.dot(u, wf_ref[...], preferred_element_type=_F32)
    logf = _log_sigmoid(fl + bf_ref[...])
    lane = lax.broadcasted_iota(jnp.int32, (1, LANES), 1)
    grp = lane // ATTN_HEADS
    row = lax.broadcasted_iota(jnp.int32, (tm, tm), 0)
    col = lax.broadcasted_iota(jnp.int32, (tm, tm), 1)
    tri = jnp.where(col <= row, 1.0, 0.0).astype(_BF16)
    c3 = jnp.dot(tri, _bf16_pieces(logf, grp), preferred_element_type=_F32)
    h8 = ATTN_HEADS
    csum = (c3 + pltpu.roll(c3, h8, axis=1) + pltpu.roll(c3, 2 * h8, axis=1)
            + pltpu.roll(c3, LANES - h8, axis=1) + pltpu.roll(c3, LANES - 2 * h8, axis=1))
    csum = csum + carry_s[...]
    carry_s[...] = csum[tm - 1:tm, :]
    aug_ref[0] = _bf16_pieces(csum * (-LOG2E), grp)
    if emit_tail:
        crow_ref[0] = csum

    gate_b = proj(2)
    cx = proj(3) * proj(4)
    cx_s[0:SUBLANES, :] = halo_s[...]
    cx_s[SUBLANES:SUBLANES + tm, :] = cx
    if emit_tail:
        cx_ref[0] = cx
    conv = (cw_ref[0:1, :] * cx_s[SUBLANES - 2:SUBLANES - 2 + tm, :]
            + cw_ref[1:2, :] * cx_s[SUBLANES - 1:SUBLANES - 1 + tm, :]
            + cw_ref[2:3, :] * cx)
    halo_s[...] = cx[tm - SUBLANES:tm, :]
    yb = gate_b * conv
    zc = proj(5)
    low = lane < HEAD_DIM
    for cb in range(D_CONV // LANES):
        sl = slice(cb * LANES, (cb + 1) * LANES)
        y = yb[:, sl]
        y2 = y * y
        s_lo = jnp.sum(jnp.where(low, y2, 0.0), axis=-1, keepdims=True)
        s_hi = jnp.sum(jnp.where(low, 0.0, y2), axis=-1, keepdims=True)
        msq = jnp.where(low, s_lo, s_hi) * (1.0 / HEAD_DIM)
        yn = y * lax.rsqrt(msq + EPS) * cg_ref[:, sl]
        yc_ref[0, :, sl] = (yn * _silu(zc[:, sl])).astype(_BF16)


def _inproj(x3, g, w_main, wqv_t, wf, bf, cw, cg, halo0, carry0, *, tm, emit_tail):
    b, l, d = x3.shape
    nt = l // tm
    kern = functools.partial(_inproj_kernel, tm=tm, emit_tail=emit_tail)
    const = lambda bi, i: (0, 0)
    row_blk = lambda bi, i: (bi, i, 0)
    col_blk = lambda bi, i: (bi, 0, i)
    rows = (jax.ShapeDtypeStruct((b, l, 512), _BF16), pl.BlockSpec((1, tm, 512), row_blk))
    cols = (jax.ShapeDtypeStruct((b, 512, l), _BF16), pl.BlockSpec((1, 512, tm), col_blk))
    out_shape, out_specs = (list(t) for t in zip(cols, rows, cols, rows, rows))
    out_shape.append(jax.ShapeDtypeStruct((b, l, LANES), _BF16))
    out_specs.append(pl.BlockSpec((1, tm, LANES), row_blk))
    if emit_tail:
        out_shape.append(jax.ShapeDtypeStruct((b, l, D_CONV), _F32))
        out_specs.append(pl.BlockSpec((1, tm, D_CONV), row_blk))
        out_shape.append(jax.ShapeDtypeStruct((b, l, LANES), _F32))
        out_specs.append(pl.BlockSpec((1, tm, LANES), row_blk))
    return pl.pallas_call(
        kern,
        grid=(b, nt),
        in_specs=[
            pl.BlockSpec((1, tm, d), row_blk),
            pl.BlockSpec((1, d), const),
            pl.BlockSpec(w_main.shape, const),
            pl.BlockSpec(wqv_t.shape, const),
            pl.BlockSpec(wf.shape, const),
            pl.BlockSpec((1, LANES), const),
            pl.BlockSpec((CONV_WIDTH, D_CONV), const),
            pl.BlockSpec((1, D_CONV), const),
            pl.BlockSpec((SUBLANES, D_CONV), const),
            pl.BlockSpec((1, LANES), const),
        ],
        out_specs=out_specs,
        out_shape=out_shape,
        scratch_shapes=[
            pltpu.VMEM((SUBLANES, D_CONV), _F32),
            pltpu.VMEM((1, LANES), _F32),
            pltpu.VMEM((tm + SUBLANES, D_CONV), _F32),
        ],
        compiler_params=pltpu.CompilerParams(
            dimension_semantics=("arbitrary", "arbitrary"),
            vmem_limit_bytes=VMEM_LIMIT),
        name="inproj_meta" if emit_tail else "inproj",
    )(x3, g, w_main, wqv_t, wf, bf, cw, cg, halo0, carry0)


def _attn_kernel(qt_ref, k_ref, aug_ref, vt_ref, km_ref, augm_ref, vmt_ref, z_ref, g_ref,
                 o_ref, q2t_s, s_buf, mb_s, m_s, l_s, acc_s, *, tq):
    tk = tq
    pair = pl.program_id(1)
    qi = pl.program_id(2)
    qt = qt_ref[0]
    sub = lax.broadcasted_iota(jnp.int32, (LANES, 1), 0)
    top = sub < HEAD_DIM
    zero = jnp.zeros_like(qt)

    def ones_for(h):
        sel = jnp.logical_and(sub % ATTN_HEADS == h, sub < N_PIECES * ATTN_HEADS)
        return jnp.broadcast_to(jnp.where(sel, 1.0, 0.0).astype(_BF16), (LANES, tq))

    q2t_s[0:LANES, 0:tq] = jnp.where(top, qt, zero)
    q2t_s[0:LANES, tq:] = jnp.where(top, zero, qt)
    q2t_s[LANES:, 0:tq] = ones_for(2 * pair)
    q2t_s[LANES:, tq:] = ones_for(2 * pair + 1)

    s = jnp.dot(jnp.concatenate([km_ref[...], augm_ref[...]], axis=1), q2t_s[...],
                preferred_element_type=_F32)
    m = jnp.max(s, axis=0, keepdims=True)
    p = jnp.exp2(s - m)
    m_s[...] = m
    l_s[...] = jnp.sum(p, axis=0, keepdims=True)
    acc_s[...] = jnp.dot(vmt_ref[...], p.astype(_BF16), preferred_element_type=_F32)

    def produce(j, slot, masked):
        start = pl.multiple_of(j * tk, tk)
        kx = jnp.concatenate([k_ref[0, pl.ds(start, tk), :],
                              aug_ref[0, pl.ds(start, tk), :]], axis=1)
        s = jnp.dot(kx, q2t_s[...], preferred_element_type=_F32)
        if masked:
            c = lax.broadcasted_iota(jnp.int32, (tk, 2 * tq), 1)
            q_loc = jnp.where(c >= tq, c - tq, c)
            k_loc = lax.broadcasted_iota(jnp.int32, (tk, 2 * tq), 0)
            s = jnp.where(k_loc <= q_loc, s, -jnp.inf)
        s_buf[slot] = s
        mb_s[slot] = jnp.max(s, axis=0, keepdims=True)

    def consume(j, slot):
        start = pl.multiple_of(j * tk, tk)
        vt = vt_ref[0, :, pl.ds(start, tk)]
        m = m_s[...]
        m_new = jnp.maximum(m, mb_s[slot])
        alpha = jnp.exp2(m - m_new)
        p = jnp.exp2(s_buf[slot] - m_new)
        m_s[...] = m_new
        l_s[...] = alpha * l_s[...] + jnp.sum(p, axis=0, keepdims=True)
        acc_s[...] = alpha * acc_s[...] + jnp.dot(vt, p.astype(_BF16),
                                                  preferred_element_type=_F32)

    n_full = qi

    @pl.when(n_full > 0)
    def _():
        produce(0, 0, False)
        n_mid = n_full - 1

        def tick_pair(u, carry):
            t = 2 * u + 1
            produce(t, 1, False)
            consume(t - 1, 0)
            produce(t + 1, 0, False)
            consume(t, 1)
            return carry

        lax.fori_loop(0, n_mid // 2, tick_pair, 0)

        @pl.when(n_mid % 2 == 1)
        def _():
            produce(n_mid, 1, False)
            consume(n_mid - 1, 0)

        consume(n_full - 1, (n_full - 1) % 2)

    produce(n_full, 0, True)
    consume(n_full, 0)

    ot = acc_s[...] * (1.0 / l_s[...])
    oc = jnp.where(top, ot[:, :tq], ot[:, tq:])
    o2 = oc * oc
    msa = jnp.sum(jnp.where(top, o2, 0.0), axis=0, keepdims=True)
    msb = jnp.sum(jnp.where(top, 0.0, o2), axis=0, keepdims=True)
    inv = jnp.where(top, lax.rsqrt(msa * (1.0 / HEAD_DIM) + EPS),
                    lax.rsqrt(msb * (1.0 / HEAD_DIM) + EPS))
    y = (oc * inv).T * g_ref[...]
    z = z_ref[0].astype(_F32)
    o_ref[0] = (y * _silu(z)).astype(_BF16)


def _attention(qt, k, aug, vt, km, augm, vmt, z, g, *, tq):
    b, l, _ = k.shape
    n_pair = D_ATTN // LANES
    kern = functools.partial(_attn_kernel, tq=tq)
    q_blk = lambda bi, p, i: (bi, i, p)
    return pl.pallas_call(
        kern,
        grid=(b, n_pair, l // tq),
        in_specs=[
            pl.BlockSpec((1, LANES, tq), lambda bi, p, i: (bi, p, i)),
            pl.BlockSpec((1, l, LANES), lambda bi, p, i: (bi, 0, p)),
            pl.BlockSpec((1, l, LANES), lambda bi, p, i: (bi, 0, 0)),
            pl.BlockSpec((1, LANES, l), lambda bi, p, i: (bi, p, 0)),
            pl.BlockSpec((N_META, LANES), lambda bi, p, i: (0, p)),
            pl.BlockSpec((N_META, LANES), lambda bi, p, i: (0, 0)),
            pl.BlockSpec((LANES, N_META), lambda bi, p, i: (p, 0)),
            pl.BlockSpec((1, tq, LANES), q_blk),
            pl.BlockSpec((1, LANES), lambda bi, p, i: (0, p)),
        ],
        out_specs=pl.BlockSpec((1, tq, LANES), q_blk),
        out_shape=jax.ShapeDtypeStruct((b, l, D_ATTN), _BF16),
        scratch_shapes=[
            pltpu.VMEM((2 * LANES, 2 * tq), _BF16),
            pltpu.VMEM((2, tq, 2 * tq), _F32),
            pltpu.VMEM((2, 1, 2 * tq), _F32),
            pltpu.VMEM((1, 2 * tq), _F32),
            pltpu.VMEM((1, 2 * tq), _F32),
            pltpu.VMEM((LANES, 2 * tq), _F32),
        ],
        compiler_params=pltpu.CompilerParams(
            dimension_semantics=("arbitrary", "arbitrary", "arbitrary"),
            vmem_limit_bytes=VMEM_LIMIT),
        name="fox_attention",
    )(qt, k, aug, vt, km, augm, vmt, z, g)


def _outproj_kernel(x_ref, ya_ref, yc_ref, w_ref, g_ref, o_ref):
    hres = (x_ref[0]
            + jnp.dot(ya_ref[0], w_ref[0:D_ATTN, :], preferred_element_type=_F32)
            + jnp.dot(yc_ref[0], w_ref[D_ATTN:, :], preferred_element_type=_F32))
    ms = jnp.mean(hres * hres, axis=-1, keepdims=True)
    o_ref[0] = hres * lax.rsqrt(ms + EPS) * g_ref[...]


def _outproj(x, ya, yc, w_out, g, *, tm):
    b, l, d = x.shape
    row_blk = lambda bi, i: (bi, i, 0)
    const = lambda bi, i: (0, 0)
    return pl.pallas_call(
        _outproj_kernel,
        grid=(b, l // tm),
        in_specs=[
            pl.BlockSpec((1, tm, d), row_blk),
            pl.BlockSpec((1, tm, D_ATTN), row_blk),
            pl.BlockSpec((1, tm, D_CONV), row_blk),
            pl.BlockSpec(w_out.shape, const),
            pl.BlockSpec((1, d), const),
        ],
        out_specs=pl.BlockSpec((1, tm, d), row_blk),
        out_shape=jax.ShapeDtypeStruct((b, l, d), _F32),
        compiler_params=pltpu.CompilerParams(
            dimension_semantics=("arbitrary", "arbitrary"),
            vmem_limit_bytes=VMEM_LIMIT),
        name="outproj",
    )(x, ya, yc, w_out, g)


def kernel(x, meta, norm_g, w_in, b_f, conv_w, attn_norm_g, conv_norm_g, w_out, final_norm_g):
    assert norm_g.shape[0] == 1, "single layer only"
    b, seq, d = x.shape
    w = w_in[0]
    f0 = 3 * D_ATTN
    n_rep = N_PIECES * ATTN_HEADS
    w_main = jnp.concatenate([w[:, D_ATTN:2 * D_ATTN], w[:, f0 + ATTN_HEADS:]],
                             axis=1).astype(_BF16)
    wqv_t = jnp.concatenate([w[:, :D_ATTN], w[:, 2 * D_ATTN:f0]], axis=1).T.astype(_BF16)
    wf8 = w[:, f0:f0 + ATTN_HEADS]
    wf = jnp.concatenate([wf8] * N_PIECES + [jnp.zeros((d, LANES - n_rep), w.dtype)],
                         axis=1).astype(_BF16)
    bf = jnp.concatenate([b_f[0]] * N_PIECES
                         + [jnp.zeros((LANES - n_rep,), b_f.dtype)])[None, :]
    cw = conv_w[0]

    meta_rows = LANES
    meta_p = jnp.pad(meta.astype(x.dtype), ((0, meta_rows - N_META), (0, 0)))[None]
    zero_halo = jnp.zeros((SUBLANES, D_CONV), _F32)
    zero_carry = jnp.zeros((1, LANES), _F32)
    _, km, vmt, _, _, augm, cxm, crow = _inproj(
        meta_p, norm_g, w_main, wqv_t, wf, bf, cw, conv_norm_g, zero_halo, zero_carry,
        tm=meta_rows, emit_tail=True)
    km = km[0, :N_META]
    vmt = vmt[0, :, :N_META]
    augm = augm[0, :N_META]
    halo0 = cxm[0, N_META - SUBLANES:N_META]
    carry0 = crow[0, N_META - 1:N_META]

    qt, k, vt, z, yc, aug = _inproj(x, norm_g, w_main, wqv_t, wf, bf, cw, conv_norm_g,
                                    halo0, carry0, tm=512, emit_tail=False)
    ya = _attention(qt, k, aug, vt, km, augm, vmt, z, attn_norm_g, tq=512)
    return _outproj(x, ya, yc, w_out[0].astype(_BF16), final_norm_g[None, :], tm=512)
```

```python
import functools
import math

import jax
import jax.numpy as jnp
from jax import lax
from jax.experimental import pallas as pl
from jax.experimental.pallas import tpu as pltpu

D_MODEL = 1024
N_META = 16
ATTN_HEADS = 8
HEAD_DIM = 64
D_ATTN = 512
D_CONV = 512
CONV_WIDTH = 3
EPS = 1e-6
LANES = 128
SUBLANES = 8
N_PIECES = 3
LOG2E = math.log2(math.e)
VMEM_LIMIT = 48 * 1024 * 1024

_BF16 = jnp.bfloat16
_F32 = jnp.float32


def _log_sigmoid(x):
    return jnp.minimum(x, 0.0) - jnp.log1p(jnp.exp(-jnp.abs(x)))


def _silu(x):
    return x * (1.0 / (1.0 + jnp.exp(-x)))


def _dot_nt(a, b):
    return lax.dot_general(a, b, (((1,), (1,)), ((), ())),
                           preferred_element_type=_F32)


def _bf16_pieces(x, grp):
    hi = x.astype(_BF16).astype(_F32)
    r1 = x - hi
    mid = r1.astype(_BF16).astype(_F32)
    lo = r1 - mid
    return jnp.where(grp == 0, hi, jnp.where(grp == 1, mid,
                     jnp.where(grp == 2, lo, 0.0))).astype(_BF16)


def _inproj_kernel(x_ref, g_ref, w_ref, wqv_ref, wf_ref, bf_ref, cw_ref, cg_ref, halo0_ref,
                   carry0_ref, qt_ref, k_ref, vt_ref, z_ref, yc_ref, aug_ref, *rest,
                   tm, emit_tail):
    if emit_tail:
        cx_ref, crow_ref, halo_s, carry_s, cx_s = rest
    else:
        halo_s, carry_s, cx_s = rest
    i = pl.program_id(1)

    @pl.when(i == 0)
    def _():
        halo_s[...] = halo0_ref[...]
        carry_s[...] = carry0_ref[...]

    x = x_ref[0]
    ms = jnp.mean(x * x, axis=-1, keepdims=True)
    u = (x * lax.rsqrt(ms + EPS) * g_ref[...]).astype(_BF16)

    def proj(c):
        return jnp.dot(u, w_ref[:, c * 512:(c + 1) * 512],
                       preferred_element_type=_F32)

    qvt = _dot_nt(wqv_ref[...], u)
    qt_ref[0] = (qvt[:D_ATTN] * (LOG2E * HEAD_DIM ** -0.5)).astype(_BF16)
    vt_ref[0] = qvt[D_ATTN:].astype(_BF16)
    k_ref[0] = proj(0).astype(_BF16)
    z_ref[0] = proj(1).astype(_BF16)

    fl = jnp.dot(u, wf_ref[...], preferred_element_type=_F32)
    logf = _log_sigmoid(fl + bf_ref[...])
    lane = lax.broadcasted_iota(jnp.int32, (1, LANES), 1)
    grp = lane // ATTN_HEADS
    row = lax.broadcasted_iota(jnp.int32, (tm, tm), 0)
    col = lax.broadcasted_iota(jnp.int32, (tm, tm), 1)
    tri = jnp.where(col <= row, 1.0, 0.0).astype(_BF16)
    c3 = jnp.dot(tri, _bf16_pieces(logf, grp), preferred_element_type=_F32)
    h8 = ATTN_HEADS
    csum = (c3 + pltpu.roll(c3, h8, axis=1) + pltpu.roll(c3, 2 * h8, axis=1)
            + pltpu.roll(c3, LANES - h8, axis=1) + pltpu.roll(c3, LANES - 2 * h8, axis=1))
    csum = csum + carry_s[...]
    carry_s[...] = csum[tm - 1:tm, :]
    aug_ref[0] = _bf16_pieces(csum * (-LOG2E), grp)
    if emit_tail:
        crow_ref[0] = csum

    gate_b = proj(2)
    cx = proj(3) * proj(4)
    cx_s[0:SUBLANES, :] = halo_s[...]
    cx_s[SUBLANES:SUBLANES + tm, :] = cx
    if emit_tail:
        cx_ref[0] = cx
    conv = (cw_ref[0:1, :] * cx_s[SUBLANES - 2:SUBLANES - 2 + tm, :]
            + cw_ref[1:2, :] * cx_s[SUBLANES - 1:SUBLANES - 1 + tm, :]
            + cw_ref[2:3, :] * cx)
    halo_s[...] = cx[tm - SUBLANES:tm, :]
    yb = gate_b * conv
    zc = proj(5)
    low = lane < HEAD_DIM
    for cb in range(D_CONV // LANES):
        sl = slice(cb * LANES, (cb + 1) * LANES)
        y = yb[:, sl]
        y2 = y * y
        s_lo = jnp.sum(jnp.where(low, y2, 0.0), axis=-1, keepdims=True)
        s_hi = jnp.sum(jnp.where(low, 0.0, y2), axis=-1, keepdims=True)
        msq = jnp.where(low, s_lo, s_hi) * (1.0 / HEAD_DIM)
        yn = y * lax.rsqrt(msq + EPS) * cg_ref[:, sl]
        yc_ref[0, :, sl] = (yn * _silu(zc[:, sl])).astype(_BF16)


def _inproj(x3, g, w_main, wqv_t, wf, bf, cw, cg, halo0, carry0, *, tm, emit_tail):
    b, l, d = x3.shape
    nt = l // tm
    kern = functools.partial(_inproj_kernel, tm=tm, emit_tail=emit_tail)
    const = lambda bi, i: (0, 0)
    row_blk = lambda bi, i: (bi, i, 0)
    col_blk = lambda bi, i: (bi, 0, i)
    rows = (jax.ShapeDtypeStruct((b, l, 512), _BF16), pl.BlockSpec((1, tm, 512), row_blk))
    cols = (jax.ShapeDtypeStruct((b, 512, l), _BF16), pl.BlockSpec((1, 512, tm), col_blk))
    out_shape, out_specs = (list(t) for t in zip(cols, rows, cols, rows, rows))
    out_shape.append(jax.ShapeDtypeStruct((b, l, LANES), _BF16))
    out_specs.append(pl.BlockSpec((1, tm, LANES), row_blk))
    if emit_tail:
        out_shape.append(jax.ShapeDtypeStruct((b, l, D_CONV), _F32))
        out_specs.append(pl.BlockSpec((1, tm, D_CONV), row_blk))
        out_shape.append(jax.ShapeDtypeStruct((b, l, LANES), _F32))
        out_specs.append(pl.BlockSpec((1, tm, LANES), row_blk))
    return pl.pallas_call(
        kern,
        grid=(b, nt),
        in_specs=[
            pl.BlockSpec((1, tm, d), row_blk),
            pl.BlockSpec((1, d), const),
            pl.BlockSpec(w_main.shape, const),
            pl.BlockSpec(wqv_t.shape, const),
            pl.BlockSpec(wf.shape, const),
            pl.BlockSpec((1, LANES), const),
            pl.BlockSpec((CONV_WIDTH, D_CONV), const),
            pl.BlockSpec((1, D_CONV), const),
            pl.BlockSpec((SUBLANES, D_CONV), const),
            pl.BlockSpec((1, LANES), const),
        ],
        out_specs=out_specs,
        out_shape=out_shape,
        scratch_shapes=[
            pltpu.VMEM((SUBLANES, D_CONV), _F32),
            pltpu.VMEM((1, LANES), _F32),
            pltpu.VMEM((tm + SUBLANES, D_CONV), _F32),
        ],
        compiler_params=pltpu.CompilerParams(
            dimension_semantics=("arbitrary", "arbitrary"),
            vmem_limit_bytes=VMEM_LIMIT),
        name="inproj_meta" if emit_tail else "inproj",
    )(x3, g, w_main, wqv_t, wf, bf, cw, cg, halo0, carry0)


_DIAG_SLOT = 2


def _attn_kernel(qt_ref, k_ref, aug_ref, vt_ref, km_ref, augm_ref, vmt_ref, z_ref, g_ref,
                 o_ref, q2t_s, s_buf, mb_s, m_s, l_s, acc_s, *, tq, n_q):
    tk = tq
    pair = pl.program_id(1)
    sub = lax.broadcasted_iota(jnp.int32, (LANES, 1), 0)
    top = sub < HEAD_DIM

    def ones_for(h):
        sel = jnp.logical_and(sub % ATTN_HEADS == h, sub < N_PIECES * ATTN_HEADS)
        return jnp.broadcast_to(jnp.where(sel, 1.0, 0.0).astype(_BF16), (LANES, tq))

    q2t_s[LANES:, 0:tq] = ones_for(2 * pair)
    q2t_s[LANES:, tq:] = ones_for(2 * pair + 1)

    def load_q(qi):
        qt = qt_ref[0, :, pl.ds(pl.multiple_of(qi * tq, tq), tq)]
        zero = jnp.zeros_like(qt)
        q2t_s[0:LANES, 0:tq] = jnp.where(top, qt, zero)
        q2t_s[0:LANES, tq:] = jnp.where(top, zero, qt)

    def init_stats():
        s = jnp.dot(jnp.concatenate([km_ref[...], augm_ref[...]], axis=1), q2t_s[...],
                    preferred_element_type=_F32)
        m = jnp.max(s, axis=0, keepdims=True)
        p = jnp.exp2(s - m)
        m_s[...] = m
        l_s[...] = jnp.sum(p, axis=0, keepdims=True)
        acc_s[...] = jnp.dot(vmt_ref[...], p.astype(_BF16), preferred_element_type=_F32)

    def produce(j, slot, masked):
        start = pl.multiple_of(j * tk, tk)
        kx = jnp.concatenate([k_ref[0, pl.ds(start, tk), :],
                              aug_ref[0, pl.ds(start, tk), :]], axis=1)
        s = jnp.dot(kx, q2t_s[...], preferred_element_type=_F32)
        if masked:
            c = lax.broadcasted_iota(jnp.int32, (tk, 2 * tq), 1)
            q_loc = jnp.where(c >= tq, c - tq, c)
            k_loc = lax.broadcasted_iota(jnp.int32, (tk, 2 * tq), 0)
            s = jnp.where(k_loc <= q_loc, s, -jnp.inf)
        s_buf[slot] = s
        mb_s[slot] = jnp.max(s, axis=0, keepdims=True)

    def consume(j, slot):
        start = pl.multiple_of(j * tk, tk)
        vt = vt_ref[0, :, pl.ds(start, tk)]
        m = m_s[...]
        m_new = jnp.maximum(m, mb_s[slot])
        alpha = jnp.exp2(m - m_new)
        p = jnp.exp2(s_buf[slot] - m_new)
        m_s[...] = m_new
        l_s[...] = alpha * l_s[...] + jnp.sum(p, axis=0, keepdims=True)
        acc_s[...] = alpha * acc_s[...] + jnp.dot(vt, p.astype(_BF16),
                                                  preferred_element_type=_F32)

    def finalize(qi):
        rows = pl.ds(pl.multiple_of(qi * tq, tq), tq)
        ot = acc_s[...] * (1.0 / l_s[...])
        oc = jnp.where(top, ot[:, :tq], ot[:, tq:])
        o2 = oc * oc
        msa = jnp.sum(jnp.where(top, o2, 0.0), axis=0, keepdims=True)
        msb = jnp.sum(jnp.where(top, 0.0, o2), axis=0, keepdims=True)
        inv = jnp.where(top, lax.rsqrt(msa * (1.0 / HEAD_DIM) + EPS),
                        lax.rsqrt(msb * (1.0 / HEAD_DIM) + EPS))
        y = (oc * inv).T * g_ref[...]
        z = z_ref[0, rows, :].astype(_F32)
        o_ref[0, rows, :] = (y * _silu(z)).astype(_BF16)

    load_q(0)
    init_stats()
    produce(0, _DIAG_SLOT, True)

    def q_block(qi, carry):
        consume(qi - 1, _DIAG_SLOT)
        finalize(qi - 1)
        load_q(qi)
        produce(0, 0, False)
        init_stats()
        n_mid = qi - 1

        def tick_pair(u, c):
            t = 2 * u + 1
            produce(t, 1, False)
            consume(t - 1, 0)
            produce(t + 1, 0, False)
            consume(t, 1)
            return c

        lax.fori_loop(0, n_mid // 2, tick_pair, 0)

        @pl.when(n_mid % 2 == 1)
        def _():
            produce(n_mid, 1, False)
            consume(n_mid - 1, 0)

        produce(qi, _DIAG_SLOT, True)
        consume(qi - 1, (qi - 1) % 2)
        return carry

    lax.fori_loop(1, n_q, q_block, 0)
    consume(n_q - 1, _DIAG_SLOT)
    finalize(n_q - 1)


def _attention(qt, k, aug, vt, km, augm, vmt, z, g, *, tq):
    b, l, _ = k.shape
    n_pair = D_ATTN // LANES
    kern = functools.partial(_attn_kernel, tq=tq, n_q=l // tq)
    seq_rows = pl.BlockSpec((1, l, LANES), lambda bi, p: (bi, 0, p))
    seq_cols = pl.BlockSpec((1, LANES, l), lambda bi, p: (bi, p, 0))
    return pl.pallas_call(
        kern,
        grid=(b, n_pair),
        in_specs=[
            seq_cols,
            seq_rows,
            pl.BlockSpec((1, l, LANES), lambda bi, p: (bi, 0, 0)),
            seq_cols,
            pl.BlockSpec((N_META, LANES), lambda bi, p: (0, p)),
            pl.BlockSpec((N_META, LANES), lambda bi, p: (0, 0)),
            pl.BlockSpec((LANES, N_META), lambda bi, p: (p, 0)),
            seq_rows,
            pl.BlockSpec((1, LANES), lambda bi, p: (0, p)),
        ],
        out_specs=seq_rows,
        out_shape=jax.ShapeDtypeStruct((b, l, D_ATTN), _BF16),
        scratch_shapes=[
            pltpu.VMEM((2 * LANES, 2 * tq), _BF16),
            pltpu.VMEM((3, tq, 2 * tq), _F32),
            pltpu.VMEM((3, 1, 2 * tq), _F32),
            pltpu.VMEM((1, 2 * tq), _F32),
            pltpu.VMEM((1, 2 * tq), _F32),
            pltpu.VMEM((LANES, 2 * tq), _F32),
        ],
        compiler_params=pltpu.CompilerParams(
            dimension_semantics=("arbitrary", "arbitrary"),
            vmem_limit_bytes=VMEM_LIMIT),
        name="fox_attention",
    )(qt, k, aug, vt, km, augm, vmt, z, g)


def _outproj_kernel(x_ref, ya_ref, yc_ref, w_ref, g_ref, o_ref):
    hres = (x_ref[0]
            + jnp.dot(ya_ref[0], w_ref[0:D_ATTN, :], preferred_element_type=_F32)
            + jnp.dot(yc_ref[0], w_ref[D_ATTN:, :], preferred_element_type=_F32))
    ms = jnp.mean(hres * hres, axis=-1, keepdims=True)
    o_ref[0] = hres * lax.rsqrt(ms + EPS) * g_ref[...]


def _outproj(x, ya, yc, w_out, g, *, tm):
    b, l, d = x.shape
    row_blk = lambda bi, i: (bi, i, 0)
    const = lambda bi, i: (0, 0)
    return pl.pallas_call(
        _outproj_kernel,
        grid=(b, l // tm),
        in_specs=[
            pl.BlockSpec((1, tm, d), row_blk),
            pl.BlockSpec((1, tm, D_ATTN), row_blk),
            pl.BlockSpec((1, tm, D_CONV), row_blk),
            pl.BlockSpec(w_out.shape, const),
            pl.BlockSpec((1, d), const),
        ],
        out_specs=pl.BlockSpec((1, tm, d), row_blk),
        out_shape=jax.ShapeDtypeStruct((b, l, d), _F32),
        compiler_params=pltpu.CompilerParams(
            dimension_semantics=("arbitrary", "arbitrary"),
            vmem_limit_bytes=VMEM_LIMIT),
        name="outproj",
    )(x, ya, yc, w_out, g)


def kernel(x, meta, norm_g, w_in, b_f, conv_w, attn_norm_g, conv_norm_g, w_out, final_norm_g):
    assert norm_g.shape[0] == 1, "single layer only"
    b, seq, d = x.shape
    w = w_in[0]
    f0 = 3 * D_ATTN
    n_rep = N_PIECES * ATTN_HEADS
    w_main = jnp.concatenate([w[:, D_ATTN:2 * D_ATTN], w[:, f0 + ATTN_HEADS:]],
                             axis=1).astype(_BF16)
    wqv_t = jnp.concatenate([w[:, :D_ATTN], w[:, 2 * D_ATTN:f0]], axis=1).T.astype(_BF16)
    wf8 = w[:, f0:f0 + ATTN_HEADS]
    wf = jnp.concatenate([wf8] * N_PIECES + [jnp.zeros((d, LANES - n_rep), w.dtype)],
                         axis=1).astype(_BF16)
    bf = jnp.concatenate([b_f[0]] * N_PIECES
                         + [jnp.zeros((LANES - n_rep,), b_f.dtype)])[None, :]
    cw = conv_w[0]

    meta_rows = LANES
    meta_p = jnp.pad(meta.astype(x.dtype), ((0, meta_rows - N_META), (0, 0)))[None]
    zero_halo = jnp.zeros((SUBLANES, D_CONV), _F32)
    zero_carry = jnp.zeros((1, LANES), _F32)
    _, km, vmt, _, _, augm, cxm, crow = _inproj(
        meta_p, norm_g, w_main, wqv_t, wf, bf, cw, conv_norm_g, zero_halo, zero_carry,
        tm=meta_rows, emit_tail=True)
    km = km[0, :N_META]
    vmt = vmt[0, :, :N_META]
    augm = augm[0, :N_META]
    halo0 = cxm[0, N_META - SUBLANES:N_META]
    carry0 = crow[0, N_META - 1:N_META]

    qt, k, vt, z, yc, aug = _inproj(x, norm_g, w_main, wqv_t, wf, bf, cw, conv_norm_g,
                                    halo0, carry0, tm=512, emit_tail=False)
    ya = _attention(qt, k, aug, vt, km, augm, vmt, z, attn_norm_g, tq=512)
    return _outproj(x, ya, yc, w_out[0].astype(_BF16), final_norm_g[None, :], tm=512)
```

```python
import functools
import math

import jax
import jax.numpy as jnp
from jax import lax
from jax.experimental import pallas as pl
from jax.experimental.pallas import tpu as pltpu

D_MODEL = 1024
N_META = 16
ATTN_HEADS = 8
HEAD_DIM = 64
D_ATTN = 512
D_CONV = 512
CONV_WIDTH = 3
EPS = 1e-6
LANES = 128
SUBLANES = 8
N_PIECES = 3
ONES_ROWS = 16
LOG2E = math.log2(math.e)
VMEM_LIMIT = 48 * 1024 * 1024

_BF16 = jnp.bfloat16
_F32 = jnp.float32


def _log_sigmoid(x):
    return jnp.minimum(x, 0.0) - jnp.log1p(jnp.exp(-jnp.abs(x)))


def _silu(x):
    return x * (1.0 / (1.0 + jnp.exp(-x)))


def _dot_nt(a, b):
    return lax.dot_general(a, b, (((1,), (1,)), ((), ())),
                           preferred_element_type=_F32)


def _bf16_pieces(x, grp):
    hi = x.astype(_BF16).astype(_F32)
    r1 = x - hi
    mid = r1.astype(_BF16).astype(_F32)
    lo = r1 - mid
    return jnp.where(grp == 0, hi, jnp.where(grp == 1, mid,
                     jnp.where(grp == 2, lo, 0.0))).astype(_BF16)


def _inproj_kernel(x_ref, g_ref, w_ref, wqv_ref, wf_ref, bf_ref, cw_ref, cg_ref, halo0_ref,
                   carry0_ref, qt_ref, k_ref, vt_ref, z_ref, yc_ref, aug_ref, *rest,
                   tm, emit_tail):
    if emit_tail:
        cx_ref, crow_ref, halo_s, carry_s, cx_s = rest
    else:
        halo_s, carry_s, cx_s = rest
    i = pl.program_id(1)

    @pl.when(i == 0)
    def _():
        halo_s[...] = halo0_ref[...]
        carry_s[...] = carry0_ref[...]

    x = x_ref[0]
    ms = jnp.mean(x * x, axis=-1, keepdims=True)
    u = (x * lax.rsqrt(ms + EPS) * g_ref[...]).astype(_BF16)

    def proj(c):
        return jnp.dot(u, w_ref[:, c * 512:(c + 1) * 512],
                       preferred_element_type=_F32)

    qvt = _dot_nt(wqv_ref[...], u)
    qt_ref[0] = (qvt[:D_ATTN] * (LOG2E * HEAD_DIM ** -0.5)).astype(_BF16)
    vt_ref[0] = qvt[D_ATTN:].astype(_BF16)
    k_ref[0] = proj(0).astype(_BF16)
    z_ref[0] = proj(1).astype(_BF16)

    fl = jnp.dot(u, wf_ref[...], preferred_element_type=_F32)
    logf = _log_sigmoid(fl + bf_ref[...])
    lane = lax.broadcasted_iota(jnp.int32, (1, LANES), 1)
    grp = lane // ATTN_HEADS
    row = lax.broadcasted_iota(jnp.int32, (tm, tm), 0)
    col = lax.broadcasted_iota(jnp.int32, (tm, tm), 1)
    tri = jnp.where(col <= row, 1.0, 0.0).astype(_BF16)
    c3 = jnp.dot(tri, _bf16_pieces(logf, grp), preferred_element_type=_F32)
    h8 = ATTN_HEADS
    csum = (c3 + pltpu.roll(c3, h8, axis=1) + pltpu.roll(c3, 2 * h8, axis=1)
            + pltpu.roll(c3, LANES - h8, axis=1) + pltpu.roll(c3, LANES - 2 * h8, axis=1))
    csum = csum + carry_s[...]
    carry_s[...] = csum[tm - 1:tm, :]
    aug_ref[0] = _bf16_pieces(csum * (-LOG2E), grp)
    if emit_tail:
        crow_ref[0] = csum

    gate_b = proj(2)
    cx = proj(3) * proj(4)
    cx_s[0:SUBLANES, :] = halo_s[...]
    cx_s[SUBLANES:SUBLANES + tm, :] = cx
    if emit_tail:
        cx_ref[0] = cx
    conv = (cw_ref[0:1, :] * cx_s[SUBLANES - 2:SUBLANES - 2 + tm, :]
            + cw_ref[1:2, :] * cx_s[SUBLANES - 1:SUBLANES - 1 + tm, :]
            + cw_ref[2:3, :] * cx)
    halo_s[...] = cx[tm - SUBLANES:tm, :]
    yb = gate_b * conv
    zc = proj(5)
    low = lane < HEAD_DIM
    for cb in range(D_CONV // LANES):
        sl = slice(cb * LANES, (cb + 1) * LANES)
        y = yb[:, sl]
        y2 = y * y
        s_lo = jnp.sum(jnp.where(low, y2, 0.0), axis=-1, keepdims=True)
        s_hi = jnp.sum(jnp.where(low, 0.0, y2), axis=-1, keepdims=True)
        msq = jnp.where(low, s_lo, s_hi) * (1.0 / HEAD_DIM)
        yn = y * lax.rsqrt(msq + EPS) * cg_ref[:, sl]
        yc_ref[0, :, sl] = (yn * _silu(zc[:, sl])).astype(_BF16)


def _inproj(x3, g, w_main, wqv_t, wf, bf, cw, cg, halo0, carry0, *, tm, emit_tail):
    b, l, d = x3.shape
    nt = l // tm
    kern = functools.partial(_inproj_kernel, tm=tm, emit_tail=emit_tail)
    const = lambda bi, i: (0, 0)
    row_blk = lambda bi, i: (bi, i, 0)
    col_blk = lambda bi, i: (bi, 0, i)
    rows = (jax.ShapeDtypeStruct((b, l, 512), _BF16), pl.BlockSpec((1, tm, 512), row_blk))
    cols = (jax.ShapeDtypeStruct((b, 512, l), _BF16), pl.BlockSpec((1, 512, tm), col_blk))
    out_shape, out_specs = (list(t) for t in zip(cols, rows, cols, rows, rows))
    out_shape.append(jax.ShapeDtypeStruct((b, l, LANES), _BF16))
    out_specs.append(pl.BlockSpec((1, tm, LANES), row_blk))
    if emit_tail:
        out_shape.append(jax.ShapeDtypeStruct((b, l, D_CONV), _F32))
        out_specs.append(pl.BlockSpec((1, tm, D_CONV), row_blk))
        out_shape.append(jax.ShapeDtypeStruct((b, l, LANES), _F32))
        out_specs.append(pl.BlockSpec((1, tm, LANES), row_blk))
    return pl.pallas_call(
        kern,
        grid=(b, nt),
        in_specs=[
            pl.BlockSpec((1, tm, d), row_blk),
            pl.BlockSpec((1, d), const),
            pl.BlockSpec(w_main.shape, const),
            pl.BlockSpec(wqv_t.shape, const),
            pl.BlockSpec(wf.shape, const),
            pl.BlockSpec((1, LANES), const),
            pl.BlockSpec((CONV_WIDTH, D_CONV), const),
            pl.BlockSpec((1, D_CONV), const),
            pl.BlockSpec((SUBLANES, D_CONV), const),
            pl.BlockSpec((1, LANES), const),
        ],
        out_specs=out_specs,
        out_shape=out_shape,
        scratch_shapes=[
            pltpu.VMEM((SUBLANES, D_CONV), _F32),
            pltpu.VMEM((1, LANES), _F32),
            pltpu.VMEM((tm + SUBLANES, D_CONV), _F32),
        ],
        compiler_params=pltpu.CompilerParams(
            dimension_semantics=("arbitrary", "arbitrary"),
            vmem_limit_bytes=VMEM_LIMIT),
        name="inproj_meta" if emit_tail else "inproj",
    )(x3, g, w_main, wqv_t, wf, bf, cw, cg, halo0, carry0)


def _attn_kernel(pq_ref, pk_ref, qt_ref, k_ref, aug_ref, vt_ref, km_ref, augm_ref, vmt_ref,
                 z_ref, g_ref, o_ref, q2t_s, sd_buf, su_buf, mbd_s, mbu_s, mask_s, m_s, acc_s,
                 *, tq, n_q, n_pairs):
    tk = tq
    pair = pl.program_id(1)
    sub = lax.broadcasted_iota(jnp.int32, (LANES, 1), 0)
    top = sub < HEAD_DIM
    ones_v = jnp.ones((ONES_ROWS, tk), _BF16)
    ones_m = jnp.ones((ONES_ROWS, N_META), _BF16)

    @pl.when(jnp.logical_and(pl.program_id(0) == 0, pair == 0))
    def _():
        c = lax.broadcasted_iota(jnp.int32, (tk, 2 * tq), 1)
        q_loc = jnp.where(c >= tq, c - tq, c)
        k_loc = lax.broadcasted_iota(jnp.int32, (tk, 2 * tq), 0)
        mask_s[0:tk, :] = jnp.where(k_loc <= q_loc, 0.0, -jnp.inf)
        mask_s[tk:, :] = jnp.zeros((N_META, 2 * tq), _F32)

    def ones_for(h):
        sel = jnp.logical_and(sub % ATTN_HEADS == h, sub < N_PIECES * ATTN_HEADS)
        return jnp.broadcast_to(jnp.where(sel, 1.0, 0.0).astype(_BF16), (LANES, tq))

    def build_q(qi, carry):
        qt = qt_ref[0, :, pl.ds(pl.multiple_of(qi * tq, tq), tq)]
        zero = jnp.zeros_like(qt)
        q2t_s[qi, 0:LANES, 0:tq] = jnp.where(top, qt, zero)
        q2t_s[qi, 0:LANES, tq:] = jnp.where(top, zero, qt)
        q2t_s[qi, LANES:, 0:tq] = ones_for(2 * pair)
        q2t_s[qi, LANES:, tq:] = ones_for(2 * pair + 1)
        return carry

    lax.fori_loop(0, n_q, build_q, 0)

    def keys(j):
        start = pl.multiple_of(j * tk, tk)
        return jnp.concatenate([k_ref[0, pl.ds(start, tk), :],
                                aug_ref[0, pl.ds(start, tk), :]], axis=1)

    def values(j):
        start = pl.multiple_of(j * tk, tk)
        return jnp.concatenate([vt_ref[0, :, pl.ds(start, tk)], ones_v], axis=0)

    def produce_diag(qi, slot):
        kx = jnp.concatenate(
            [keys(qi), jnp.concatenate([km_ref[...], augm_ref[...]], axis=1)], axis=0)
        s = jnp.dot(kx, q2t_s[qi], preferred_element_type=_F32) + mask_s[...]
        sd_buf[slot] = s
        mbd_s[slot] = jnp.max(s, axis=0, keepdims=True)

    def consume_diag(qi, slot):
        m = mbd_s[slot]
        p = jnp.exp2(sd_buf[slot] - m).astype(_BF16)
        m_s[qi] = m
        acc_s[qi] = (jnp.dot(values(qi), p[0:tk], preferred_element_type=_F32)
                     + jnp.dot(jnp.concatenate([vmt_ref[...], ones_m], axis=0), p[tk:],
                               preferred_element_type=_F32))

    def produce(t, slot):
        s = jnp.dot(keys(pk_ref[t]), q2t_s[pq_ref[t]], preferred_element_type=_F32)
        su_buf[slot] = s
        mbu_s[slot] = jnp.max(s, axis=0, keepdims=True)

    def consume(t, slot):
        qi = pq_ref[t]
        m = m_s[qi]
        m_new = jnp.maximum(m, mbu_s[slot])
        alpha = jnp.exp2(m - m_new)
        p = jnp.exp2(su_buf[slot] - m_new).astype(_BF16)
        m_s[qi] = m_new
        acc_s[qi] = alpha * acc_s[qi] + jnp.dot(values(pk_ref[t]), p,
                                                preferred_element_type=_F32)

    def finalize(qi, carry):
        rows = pl.ds(pl.multiple_of(qi * tq, tq), tq)
        acc = acc_s[qi]
        ot = acc[0:LANES] * (1.0 / acc[LANES:LANES + 1])
        oc = jnp.where(top, ot[:, :tq], ot[:, tq:])
        o2 = oc * oc
        msa = jnp.sum(jnp.where(top, o2, 0.0), axis=0, keepdims=True)
        msb = jnp.sum(jnp.where(top, 0.0, o2), axis=0, keepdims=True)
        inv = jnp.where(top, lax.rsqrt(msa * (1.0 / HEAD_DIM) + EPS),
                        lax.rsqrt(msb * (1.0 / HEAD_DIM) + EPS))
        y = (oc * inv).T * g_ref[...]
        z = z_ref[0, rows, :].astype(_F32)
        o_ref[0, rows, :] = (y * _silu(z)).astype(_BF16)
        return carry

    produce_diag(0, 0)

    def diag_pair(u, carry):
        i = 2 * u + 1
        produce_diag(i, 1)
        consume_diag(i - 1, 0)
        produce_diag(i + 1, 0)
        consume_diag(i, 1)
        return carry

    lax.fori_loop(0, (n_q - 2) // 2, diag_pair, 0)
    produce_diag(n_q - 1, 1)
    consume_diag(n_q - 2, 0)

    produce(0, 0)
    consume_diag(n_q - 1, 1)

    def pair_pair(u, carry):
        t = 2 * u + 1
        produce(t, 1)
        consume(t - 1, 0)
        produce(t + 1, 0)
        consume(t, 1)
        return carry

    lax.fori_loop(0, (n_pairs - 2) // 2, pair_pair, 0)
    produce(n_pairs - 1, 1)
    consume(n_pairs - 2, 0)
    consume(n_pairs - 1, 1)

    lax.fori_loop(0, n_q, finalize, 0)


def _attention(qt, k, aug, vt, km, augm, vmt, z, g, *, tq):
    b, l, _ = k.shape
    n_pair = D_ATTN // LANES
    n_q = l // tq
    assert n_q % 2 == 0 and n_q >= 2
    pq = [qi for qi in range(n_q) for _ in range(qi)]
    pk = [j for qi in range(n_q) for j in range(qi)]
    n_pairs = len(pq)
    assert n_pairs % 2 == 0 and n_pairs >= 2
    kern = functools.partial(_attn_kernel, tq=tq, n_q=n_q, n_pairs=n_pairs)
    seq_rows = pl.BlockSpec((1, l, LANES), lambda bi, p, *_: (bi, 0, p))
    seq_cols = pl.BlockSpec((1, LANES, l), lambda bi, p, *_: (bi, p, 0))
    rows_d = tq + N_META
    grid_spec = pltpu.PrefetchScalarGridSpec(
        num_scalar_prefetch=2,
        grid=(b, n_pair),
        in_specs=[
            seq_cols,
            seq_rows,
            pl.BlockSpec((1, l, LANES), lambda bi, p, *_: (bi, 0, 0)),
            seq_cols,
            pl.BlockSpec((N_META, LANES), lambda bi, p, *_: (0, p)),
            pl.BlockSpec((N_META, LANES), lambda bi, p, *_: (0, 0)),
            pl.BlockSpec((LANES, N_META), lambda bi, p, *_: (p, 0)),
            seq_rows,
            pl.BlockSpec((1, LANES), lambda bi, p, *_: (0, p)),
        ],
        out_specs=seq_rows,
        scratch_shapes=[
            pltpu.VMEM((n_q, 2 * LANES, 2 * tq), _BF16),
            pltpu.VMEM((2, rows_d, 2 * tq), _F32),
            pltpu.VMEM((2, tq, 2 * tq), _F32),
            pltpu.VMEM((2, 1, 2 * tq), _F32),
            pltpu.VMEM((2, 1, 2 * tq), _F32),
            pltpu.VMEM((rows_d, 2 * tq), _F32),
            pltpu.VMEM((n_q, 1, 2 * tq), _F32),
            pltpu.VMEM((n_q, LANES + ONES_ROWS, 2 * tq), _F32),
        ],
    )
    return pl.pallas_call(
        kern,
        grid_spec=grid_spec,
        out_shape=jax.ShapeDtypeStruct((b, l, D_ATTN), _BF16),
        compiler_params=pltpu.CompilerParams(
            dimension_semantics=("arbitrary", "arbitrary"),
            vmem_limit_bytes=VMEM_LIMIT),
        name="fox_attention",
    )(jnp.asarray(pq, jnp.int32), jnp.asarray(pk, jnp.int32),
      qt, k, aug, vt, km, augm, vmt, z, g)


def _outproj_kernel(x_ref, ya_ref, yc_ref, w_ref, g_ref, o_ref):
    hres = (x_ref[0]
            + jnp.dot(ya_ref[0], w_ref[0:D_ATTN, :], preferred_element_type=_F32)
            + jnp.dot(yc_ref[0], w_ref[D_ATTN:, :], preferred_element_type=_F32))
    ms = jnp.mean(hres * hres, axis=-1, keepdims=True)
    o_ref[0] = hres * lax.rsqrt(ms + EPS) * g_ref[...]


def _outproj(x, ya, yc, w_out, g, *, tm):
    b, l, d = x.shape
    row_blk = lambda bi, i: (bi, i, 0)
    const = lambda bi, i: (0, 0)
    return pl.pallas_call(
        _outproj_kernel,
        grid=(b, l // tm),
        in_specs=[
            pl.BlockSpec((1, tm, d), row_blk),
            pl.BlockSpec((1, tm, D_ATTN), row_blk),
            pl.BlockSpec((1, tm, D_CONV), row_blk),
            pl.BlockSpec(w_out.shape, const),
            pl.BlockSpec((1, d), const),
        ],
        out_specs=pl.BlockSpec((1, tm, d), row_blk),
        out_shape=jax.ShapeDtypeStruct((b, l, d), _F32),
        compiler_params=pltpu.CompilerParams(
            dimension_semantics=("arbitrary", "arbitrary"),
            vmem_limit_bytes=VMEM_LIMIT),
        name="outproj",
    )(x, ya, yc, w_out, g)


def kernel(x, meta, norm_g, w_in, b_f, conv_w, attn_norm_g, conv_norm_g, w_out, final_norm_g):
    assert norm_g.shape[0] == 1, "single layer only"
    b, seq, d = x.shape
    w = w_in[0]
    f0 = 3 * D_ATTN
    n_rep = N_PIECES * ATTN_HEADS
    w_main = jnp.concatenate([w[:, D_ATTN:2 * D_ATTN], w[:, f0 + ATTN_HEADS:]],
                             axis=1).astype(_BF16)
    wqv_t = jnp.concatenate([w[:, :D_ATTN], w[:, 2 * D_ATTN:f0]], axis=1).T.astype(_BF16)
    wf8 = w[:, f0:f0 + ATTN_HEADS]
    wf = jnp.concatenate([wf8] * N_PIECES + [jnp.zeros((d, LANES - n_rep), w.dtype)],
                         axis=1).astype(_BF16)
    bf = jnp.concatenate([b_f[0]] * N_PIECES
                         + [jnp.zeros((LANES - n_rep,), b_f.dtype)])[None, :]
    cw = conv_w[0]

    meta_rows = LANES
    meta_p = jnp.pad(meta.astype(x.dtype), ((0, meta_rows - N_META), (0, 0)))[None]
    zero_halo = jnp.zeros((SUBLANES, D_CONV), _F32)
    zero_carry = jnp.zeros((1, LANES), _F32)
    _, km, vmt, _, _, augm, cxm, crow = _inproj(
        meta_p, norm_g, w_main, wqv_t, wf, bf, cw, conv_norm_g, zero_halo, zero_carry,
        tm=meta_rows, emit_tail=True)
    km = km[0, :N_META]
    vmt = vmt[0, :, :N_META]
    augm = augm[0, :N_META]
    halo0 = cxm[0, N_META - SUBLANES:N_META]
    carry0 = crow[0, N_META - 1:N_META]

    qt, k, vt, z, yc, aug = _inproj(x, norm_g, w_main, wqv_t, wf, bf, cw, conv_norm_g,
                                    halo0, carry0, tm=512, emit_tail=False)
    ya = _attention(qt, k, aug, vt, km, augm, vmt, z, attn_norm_g, tq=512)
    return _outproj(x, ya, yc, w_out[0].astype(_BF16), final_norm_g[None, :], tm=512)
```

```python
import functools
import math

import jax
import jax.numpy as jnp
from jax import lax
from jax.experimental import pallas as pl
from jax.experimental.pallas import tpu as pltpu

D_MODEL = 1024
N_META = 16
ATTN_HEADS = 8
HEAD_DIM = 64
D_ATTN = 512
D_CONV = 512
CONV_WIDTH = 3
EPS = 1e-6
LANES = 128
SUBLANES = 8
N_PIECES = 3
ONES_ROWS = 16
TICKS_PER_TRIP = 4
LOG2E = math.log2(math.e)
VMEM_LIMIT = 48 * 1024 * 1024

_BF16 = jnp.bfloat16
_F32 = jnp.float32


def _log_sigmoid(x):
    return jnp.minimum(x, 0.0) - jnp.log1p(jnp.exp(-jnp.abs(x)))


def _silu(x):
    return x * (1.0 / (1.0 + jnp.exp(-x)))


def _dot_nt(a, b):
    return lax.dot_general(a, b, (((1,), (1,)), ((), ())),
                           preferred_element_type=_F32)


def _bf16_pieces(x, grp):
    hi = x.astype(_BF16).astype(_F32)
    r1 = x - hi
    mid = r1.astype(_BF16).astype(_F32)
    lo = r1 - mid
    return jnp.where(grp == 0, hi, jnp.where(grp == 1, mid,
                     jnp.where(grp == 2, lo, 0.0))).astype(_BF16)


def _inproj_kernel(x_ref, g_ref, w_ref, wqv_ref, wf_ref, bf_ref, cw_ref, cg_ref, halo0_ref,
                   carry0_ref, qt_ref, k_ref, vt_ref, z_ref, yc_ref, aug_ref, *rest,
                   tm, emit_tail):
    if emit_tail:
        cx_ref, crow_ref, halo_s, carry_s, cx_s = rest
    else:
        halo_s, carry_s, cx_s = rest
    i = pl.program_id(1)

    @pl.when(i == 0)
    def _():
        halo_s[...] = halo0_ref[...]
        carry_s[...] = carry0_ref[...]

    x = x_ref[0]
    ms = jnp.mean(x * x, axis=-1, keepdims=True)
    u = (x * lax.rsqrt(ms + EPS) * g_ref[...]).astype(_BF16)

    def proj(c):
        return jnp.dot(u, w_ref[:, c * 512:(c + 1) * 512],
                       preferred_element_type=_F32)

    fl = jnp.dot(u, wf_ref[...], preferred_element_type=_F32)
    logf = _log_sigmoid(fl + bf_ref[...])
    lane = lax.broadcasted_iota(jnp.int32, (1, LANES), 1)
    grp = lane // ATTN_HEADS
    row = lax.broadcasted_iota(jnp.int32, (tm, tm), 0)
    col = lax.broadcasted_iota(jnp.int32, (tm, tm), 1)
    tri = jnp.where(col <= row, 1.0, 0.0).astype(_BF16)
    c3 = jnp.dot(tri, _bf16_pieces(logf, grp), preferred_element_type=_F32)
    h8 = ATTN_HEADS
    csum = (c3 + pltpu.roll(c3, h8, axis=1) + pltpu.roll(c3, 2 * h8, axis=1)
            + pltpu.roll(c3, LANES - h8, axis=1) + pltpu.roll(c3, LANES - 2 * h8, axis=1))
    csum = csum + carry_s[...]
    carry_s[...] = csum[tm - 1:tm, :]
    aug_ref[0] = _bf16_pieces(csum * (-LOG2E), grp)
    if emit_tail:
        crow_ref[0] = csum

    gate_b = proj(2)
    cx = proj(3) * proj(4)
    cx_s[0:SUBLANES, :] = halo_s[...]
    cx_s[SUBLANES:SUBLANES + tm, :] = cx
    if emit_tail:
        cx_ref[0] = cx
    conv = (cw_ref[0:1, :] * cx_s[SUBLANES - 2:SUBLANES - 2 + tm, :]
            + cw_ref[1:2, :] * cx_s[SUBLANES - 1:SUBLANES - 1 + tm, :]
            + cw_ref[2:3, :] * cx)
    halo_s[...] = cx[tm - SUBLANES:tm, :]
    yb = gate_b * conv
    zc = proj(5)
    low = lane < HEAD_DIM
    for cb in range(D_CONV // LANES):
        sl = slice(cb * LANES, (cb + 1) * LANES)
        y = yb[:, sl]
        y2 = y * y
        s_lo = jnp.sum(jnp.where(low, y2, 0.0), axis=-1, keepdims=True)
        s_hi = jnp.sum(jnp.where(low, 0.0, y2), axis=-1, keepdims=True)
        msq = jnp.where(low, s_lo, s_hi) * (1.0 / HEAD_DIM)
        yn = y * lax.rsqrt(msq + EPS) * cg_ref[:, sl]
        yc_ref[0, :, sl] = (yn * _silu(zc[:, sl])).astype(_BF16)

    qvt = _dot_nt(wqv_ref[...], u)
    qt_ref[0] = (qvt[:D_ATTN] * (LOG2E * HEAD_DIM ** -0.5)).astype(_BF16)
    vt_ref[0] = qvt[D_ATTN:].astype(_BF16)
    k_ref[0] = proj(0).astype(_BF16)
    z_ref[0] = proj(1).astype(_BF16)


def _inproj(x3, g, w_main, wqv_t, wf, bf, cw, cg, halo0, carry0, *, tm, emit_tail):
    b, l, d = x3.shape
    nt = l // tm
    kern = functools.partial(_inproj_kernel, tm=tm, emit_tail=emit_tail)
    const = lambda bi, i: (0, 0)
    row_blk = lambda bi, i: (bi, i, 0)
    col_blk = lambda bi, i: (bi, 0, i)
    rows = (jax.ShapeDtypeStruct((b, l, 512), _BF16), pl.BlockSpec((1, tm, 512), row_blk))
    cols = (jax.ShapeDtypeStruct((b, 512, l), _BF16), pl.BlockSpec((1, 512, tm), col_blk))
    out_shape, out_specs = (list(t) for t in zip(cols, rows, cols, rows, rows))
    out_shape.append(jax.ShapeDtypeStruct((b, l, LANES), _BF16))
    out_specs.append(pl.BlockSpec((1, tm, LANES), row_blk))
    if emit_tail:
        out_shape.append(jax.ShapeDtypeStruct((b, l, D_CONV), _F32))
        out_specs.append(pl.BlockSpec((1, tm, D_CONV), row_blk))
        out_shape.append(jax.ShapeDtypeStruct((b, l, LANES), _F32))
        out_specs.append(pl.BlockSpec((1, tm, LANES), row_blk))
    return pl.pallas_call(
        kern,
        grid=(b, nt),
        in_specs=[
            pl.BlockSpec((1, tm, d), row_blk),
            pl.BlockSpec((1, d), const),
            pl.BlockSpec(w_main.shape, const),
            pl.BlockSpec(wqv_t.shape, const),
            pl.BlockSpec(wf.shape, const),
            pl.BlockSpec((1, LANES), const),
            pl.BlockSpec((CONV_WIDTH, D_CONV), const),
            pl.BlockSpec((1, D_CONV), const),
            pl.BlockSpec((SUBLANES, D_CONV), const),
            pl.BlockSpec((1, LANES), const),
        ],
        out_specs=out_specs,
        out_shape=out_shape,
        scratch_shapes=[
            pltpu.VMEM((SUBLANES, D_CONV), _F32),
            pltpu.VMEM((1, LANES), _F32),
            pltpu.VMEM((tm + SUBLANES, D_CONV), _F32),
        ],
        compiler_params=pltpu.CompilerParams(
            dimension_semantics=("arbitrary", "arbitrary"),
            vmem_limit_bytes=VMEM_LIMIT),
        name="inproj_meta" if emit_tail else "inproj",
    )(x3, g, w_main, wqv_t, wf, bf, cw, cg, halo0, carry0)


def _attn_kernel(pq_ref, pk_ref, qt_ref, k_ref, aug_ref, vt_ref, km_ref, augm_ref, vmt_ref,
                 z_ref, g_ref, o_ref, q2t_s, sd_buf, su_buf, mbd_s, mbu_s, mask_s, m_s, acc_s,
                 *, tq, n_q, n_pairs):
    tk = tq
    pair = pl.program_id(1)
    sub = lax.broadcasted_iota(jnp.int32, (LANES, 1), 0)
    top = sub < HEAD_DIM
    ones_v = jnp.ones((ONES_ROWS, tk), _BF16)
    ones_m = jnp.ones((ONES_ROWS, N_META), _BF16)

    @pl.when(jnp.logical_and(pl.program_id(0) == 0, pair == 0))
    def _():
        c = lax.broadcasted_iota(jnp.int32, (tk, 2 * tq), 1)
        q_loc = jnp.where(c >= tq, c - tq, c)
        k_loc = lax.broadcasted_iota(jnp.int32, (tk, 2 * tq), 0)
        mask_s[0:tk, :] = jnp.where(k_loc <= q_loc, 0.0, -jnp.inf)
        mask_s[tk:, :] = jnp.zeros((N_META, 2 * tq), _F32)

    def ones_for(h):
        sel = jnp.logical_and(sub % ATTN_HEADS == h, sub < N_PIECES * ATTN_HEADS)
        return jnp.broadcast_to(jnp.where(sel, 1.0, 0.0).astype(_BF16), (LANES, tq))

    def build_q(qi, carry):
        qt = qt_ref[0, :, pl.ds(pl.multiple_of(qi * tq, tq), tq)]
        zero = jnp.zeros_like(qt)
        q2t_s[qi, 0:LANES, 0:tq] = jnp.where(top, qt, zero)
        q2t_s[qi, 0:LANES, tq:] = jnp.where(top, zero, qt)
        q2t_s[qi, LANES:, 0:tq] = ones_for(2 * pair)
        q2t_s[qi, LANES:, tq:] = ones_for(2 * pair + 1)
        return carry

    lax.fori_loop(0, n_q, build_q, 0)

    def keys(j):
        start = pl.multiple_of(j * tk, tk)
        return jnp.concatenate([k_ref[0, pl.ds(start, tk), :],
                                aug_ref[0, pl.ds(start, tk), :]], axis=1)

    def values(j):
        start = pl.multiple_of(j * tk, tk)
        return jnp.concatenate([vt_ref[0, :, pl.ds(start, tk)], ones_v], axis=0)

    def produce_diag(qi, slot):
        kx = jnp.concatenate(
            [keys(qi), jnp.concatenate([km_ref[...], augm_ref[...]], axis=1)], axis=0)
        s = jnp.dot(kx, q2t_s[qi], preferred_element_type=_F32) + mask_s[...]
        sd_buf[slot] = s
        mbd_s[slot] = jnp.max(s, axis=0, keepdims=True)

    def consume_diag(qi, slot):
        m = mbd_s[slot]
        p = jnp.exp2(sd_buf[slot] - m).astype(_BF16)
        m_s[qi] = m
        acc_s[qi] = (jnp.dot(values(qi), p[0:tk], preferred_element_type=_F32)
                     + jnp.dot(jnp.concatenate([vmt_ref[...], ones_m], axis=0), p[tk:],
                               preferred_element_type=_F32))

    def produce(t, slot):
        s = jnp.dot(keys(pk_ref[t]), q2t_s[pq_ref[t]], preferred_element_type=_F32)
        su_buf[slot] = s
        mbu_s[slot] = jnp.max(s, axis=0, keepdims=True)

    def consume(t, slot):
        qi = pq_ref[t]
        m = m_s[qi]
        m_new = jnp.maximum(m, mbu_s[slot])
        alpha = jnp.exp2(m - m_new)
        p = jnp.exp2(su_buf[slot] - m_new).astype(_BF16)
        m_s[qi] = m_new
        acc_s[qi] = alpha * acc_s[qi] + jnp.dot(values(pk_ref[t]), p,
                                                preferred_element_type=_F32)

    def finalize(qi, carry):
        rows = pl.ds(pl.multiple_of(qi * tq, tq), tq)
        acc = acc_s[qi]
        ot = acc[0:LANES] * (1.0 / acc[LANES:LANES + 1])
        oc = jnp.where(top, ot[:, :tq], ot[:, tq:])
        o2 = oc * oc
        msa = jnp.sum(jnp.where(top, o2, 0.0), axis=0, keepdims=True)
        msb = jnp.sum(jnp.where(top, 0.0, o2), axis=0, keepdims=True)
        inv = jnp.where(top, lax.rsqrt(msa * (1.0 / HEAD_DIM) + EPS),
                        lax.rsqrt(msb * (1.0 / HEAD_DIM) + EPS))
        y = (oc * inv).T * g_ref[...]
        z = z_ref[0, rows, :].astype(_F32)
        o_ref[0, rows, :] = (y * _silu(z)).astype(_BF16)
        return carry

    produce_diag(0, 0)
    for qi in range(1, n_q):
        produce_diag(qi, qi % 2)
        consume_diag(qi - 1, (qi - 1) % 2)

    produce(0, 0)
    consume_diag(n_q - 1, (n_q - 1) % 2)

    def ticks(t0, n):
        for d in range(n):
            produce(t0 + d, (1 + d) % 2)
            consume(t0 + d - 1, d % 2)

    def tick_group(u, carry):
        ticks(TICKS_PER_TRIP * u + 1, TICKS_PER_TRIP)
        return carry

    n_trips = (n_pairs - 1) // TICKS_PER_TRIP
    lax.fori_loop(0, n_trips, tick_group, 0)
    ticks(TICKS_PER_TRIP * n_trips + 1, (n_pairs - 1) % TICKS_PER_TRIP)
    consume(n_pairs - 1, (n_pairs - 1) % 2)

    lax.fori_loop(0, n_q, finalize, 0)


def _attention(qt, k, aug, vt, km, augm, vmt, z, g, *, tq):
    b, l, _ = k.shape
    n_pair = D_ATTN // LANES
    n_q = l // tq
    assert n_q % 2 == 0 and n_q >= 2
    pq = [qi for qi in range(n_q) for _ in range(qi)]
    pk = [j for qi in range(n_q) for j in range(qi)]
    n_pairs = len(pq)
    assert n_pairs % 2 == 0 and n_pairs >= 2
    kern = functools.partial(_attn_kernel, tq=tq, n_q=n_q, n_pairs=n_pairs)
    seq_rows = pl.BlockSpec((1, l, LANES), lambda bi, p, *_: (bi, 0, p))
    seq_cols = pl.BlockSpec((1, LANES, l), lambda bi, p, *_: (bi, p, 0))
    rows_d = tq + N_META
    grid_spec = pltpu.PrefetchScalarGridSpec(
        num_scalar_prefetch=2,
        grid=(b, n_pair),
        in_specs=[
            seq_cols,
            seq_rows,
            pl.BlockSpec((1, l, LANES), lambda bi, p, *_: (bi, 0, 0)),
            seq_cols,
            pl.BlockSpec((N_META, LANES), lambda bi, p, *_: (0, p)),
            pl.BlockSpec((N_META, LANES), lambda bi, p, *_: (0, 0)),
            pl.BlockSpec((LANES, N_META), lambda bi, p, *_: (p, 0)),
            seq_rows,
            pl.BlockSpec((1, LANES), lambda bi, p, *_: (0, p)),
        ],
        out_specs=seq_rows,
        scratch_shapes=[
            pltpu.VMEM((n_q, 2 * LANES, 2 * tq), _BF16),
            pltpu.VMEM((2, rows_d, 2 * tq), _F32),
            pltpu.VMEM((2, tq, 2 * tq), _F32),
            pltpu.VMEM((2, 1, 2 * tq), _F32),
            pltpu.VMEM((2, 1, 2 * tq), _F32),
            pltpu.VMEM((rows_d, 2 * tq), _F32),
            pltpu.VMEM((n_q, 1, 2 * tq), _F32),
            pltpu.VMEM((n_q, LANES + ONES_ROWS, 2 * tq), _F32),
        ],
    )
    return pl.pallas_call(
        kern,
        grid_spec=grid_spec,
        out_shape=jax.ShapeDtypeStruct((b, l, D_ATTN), _BF16),
        compiler_params=pltpu.CompilerParams(
            dimension_semantics=("arbitrary", "arbitrary"),
            vmem_limit_bytes=VMEM_LIMIT),
        name="fox_attention",
    )(jnp.asarray(pq, jnp.int32), jnp.asarray(pk, jnp.int32),
      qt, k, aug, vt, km, augm, vmt, z, g)


def _outproj_kernel(x_ref, ya_ref, yc_ref, w_ref, g_ref, o_ref):
    hres = (x_ref[0]
            + jnp.dot(ya_ref[0], w_ref[0:D_ATTN, :], preferred_element_type=_F32)
            + jnp.dot(yc_ref[0], w_ref[D_ATTN:, :], preferred_element_type=_F32))
    ms = jnp.mean(hres * hres, axis=-1, keepdims=True)
    o_ref[0] = hres * lax.rsqrt(ms + EPS) * g_ref[...]


def _outproj(x, ya, yc, w_out, g, *, tm):
    b, l, d = x.shape
    row_blk = lambda bi, i: (bi, i, 0)
    const = lambda bi, i: (0, 0)
    return pl.pallas_call(
        _outproj_kernel,
        grid=(b, l // tm),
        in_specs=[
            pl.BlockSpec((1, tm, d), row_blk),
            pl.BlockSpec((1, tm, D_ATTN), row_blk),
            pl.BlockSpec((1, tm, D_CONV), row_blk),
            pl.BlockSpec(w_out.shape, const),
            pl.BlockSpec((1, d), const),
        ],
        out_specs=pl.BlockSpec((1, tm, d), row_blk),
        out_shape=jax.ShapeDtypeStruct((b, l, d), _F32),
        compiler_params=pltpu.CompilerParams(
            dimension_semantics=("arbitrary", "arbitrary"),
            vmem_limit_bytes=VMEM_LIMIT),
        name="outproj",
    )(x, ya, yc, w_out, g)


def kernel(x, meta, norm_g, w_in, b_f, conv_w, attn_norm_g, conv_norm_g, w_out, final_norm_g):
    assert norm_g.shape[0] == 1, "single layer only"
    b, seq, d = x.shape
    w = w_in[0]
    f0 = 3 * D_ATTN
    n_rep = N_PIECES * ATTN_HEADS
    w_main = jnp.concatenate([w[:, D_ATTN:2 * D_ATTN], w[:, f0 + ATTN_HEADS:]],
                             axis=1).astype(_BF16)
    wqv_t = jnp.concatenate([w[:, :D_ATTN], w[:, 2 * D_ATTN:f0]], axis=1).T.astype(_BF16)
    wf8 = w[:, f0:f0 + ATTN_HEADS]
    wf = jnp.concatenate([wf8] * N_PIECES + [jnp.zeros((d, LANES - n_rep), w.dtype)],
                         axis=1).astype(_BF16)
    bf = jnp.concatenate([b_f[0]] * N_PIECES
                         + [jnp.zeros((LANES - n_rep,), b_f.dtype)])[None, :]
    cw = conv_w[0]

    meta_rows = LANES
    meta_p = jnp.pad(meta.astype(x.dtype), ((0, meta_rows - N_META), (0, 0)))[None]
    zero_halo = jnp.zeros((SUBLANES, D_CONV), _F32)
    zero_carry = jnp.zeros((1, LANES), _F32)
    _, km, vmt, _, _, augm, cxm, crow = _inproj(
        meta_p, norm_g, w_main, wqv_t, wf, bf, cw, conv_norm_g, zero_halo, zero_carry,
        tm=meta_rows, emit_tail=True)
    km = km[0, :N_META]
    vmt = vmt[0, :, :N_META]
    augm = augm[0, :N_META]
    halo0 = cxm[0, N_META - SUBLANES:N_META]
    carry0 = crow[0, N_META - 1:N_META]

    qt, k, vt, z, yc, aug = _inproj(x, norm_g, w_main, wqv_t, wf, bf, cw, conv_norm_g,
                                    halo0, carry0, tm=512, emit_tail=False)
    ya = _attention(qt, k, aug, vt, km, augm, vmt, z, attn_norm_g, tq=512)
    return _outproj(x, ya, yc, w_out[0].astype(_BF16), final_norm_g[None, :], tm=1024)
```

```python
import functools
import math

import jax
import jax.numpy as jnp
from jax import lax
from jax.experimental import pallas as pl
from jax.experimental.pallas import tpu as pltpu

D_MODEL = 1024
N_META = 16
ATTN_HEADS = 8
HEAD_DIM = 64
D_ATTN = 512
D_CONV = 512
CONV_WIDTH = 3
EPS = 1e-6
LANES = 128
SUBLANES = 8
N_PIECES = 3
ONES_ROWS = 16
TICKS_PER_TRIP = 4
LOG2E = math.log2(math.e)
VMEM_LIMIT = 48 * 1024 * 1024

_BF16 = jnp.bfloat16
_F32 = jnp.float32


def _log_sigmoid(x):
    return jnp.minimum(x, 0.0) - jnp.log1p(jnp.exp(-jnp.abs(x)))


def _silu(x):
    return x * (1.0 / (1.0 + jnp.exp(-x)))


def _dot_nt(a, b):
    return lax.dot_general(a, b, (((1,), (1,)), ((), ())),
                           preferred_element_type=_F32)


def _bf16_pieces(x, grp):
    hi = x.astype(_BF16).astype(_F32)
    r1 = x - hi
    mid = r1.astype(_BF16).astype(_F32)
    lo = r1 - mid
    return jnp.where(grp == 0, hi, jnp.where(grp == 1, mid,
                     jnp.where(grp == 2, lo, 0.0))).astype(_BF16)


def _inproj_kernel(x_ref, g_ref, w_ref, wqv_ref, wf_ref, bf_ref, cw_ref, cg_ref, halo0_ref,
                   carry0_ref, qt_ref, k_ref, vt_ref, z_ref, yc_ref, aug_ref, *rest,
                   tm, emit_tail):
    if emit_tail:
        cx_ref, crow_ref, halo_s, carry_s, cx_s = rest
    else:
        halo_s, carry_s, cx_s = rest
    i = pl.program_id(1)

    @pl.when(i == 0)
    def _():
        halo_s[...] = halo0_ref[...]
        carry_s[...] = carry0_ref[...]

    x = x_ref[0]
    ms = jnp.mean(x * x, axis=-1, keepdims=True)
    u = (x * lax.rsqrt(ms + EPS) * g_ref[...]).astype(_BF16)

    def proj(c):
        return jnp.dot(u, w_ref[:, c * 512:(c + 1) * 512],
                       preferred_element_type=_F32)

    fl = jnp.dot(u, wf_ref[...], preferred_element_type=_F32)
    logf = _log_sigmoid(fl + bf_ref[...])
    lane = lax.broadcasted_iota(jnp.int32, (1, LANES), 1)
    grp = lane // ATTN_HEADS
    row = lax.broadcasted_iota(jnp.int32, (tm, tm), 0)
    col = lax.broadcasted_iota(jnp.int32, (tm, tm), 1)
    tri = jnp.where(col <= row, 1.0, 0.0).astype(_BF16)
    c3 = jnp.dot(tri, _bf16_pieces(logf, grp), preferred_element_type=_F32)
    h8 = ATTN_HEADS
    csum = (c3 + pltpu.roll(c3, h8, axis=1) + pltpu.roll(c3, 2 * h8, axis=1)
            + pltpu.roll(c3, LANES - h8, axis=1) + pltpu.roll(c3, LANES - 2 * h8, axis=1))
    csum = csum + carry_s[...]
    carry_s[...] = csum[tm - 1:tm, :]
    aug_ref[0] = _bf16_pieces(csum * (-LOG2E), grp)
    if emit_tail:
        crow_ref[0] = csum

    gate_b = proj(2)
    cx = proj(3) * proj(4)
    cx_s[0:SUBLANES, :] = halo_s[...]
    cx_s[SUBLANES:SUBLANES + tm, :] = cx
    if emit_tail:
        cx_ref[0] = cx
    conv = (cw_ref[0:1, :] * cx_s[SUBLANES - 2:SUBLANES - 2 + tm, :]
            + cw_ref[1:2, :] * cx_s[SUBLANES - 1:SUBLANES - 1 + tm, :]
            + cw_ref[2:3, :] * cx)
    halo_s[...] = cx[tm - SUBLANES:tm, :]
    yb = gate_b * conv
    zc = proj(5)
    low = lane < HEAD_DIM
    for cb in range(D_CONV // LANES):
        sl = slice(cb * LANES, (cb + 1) * LANES)
        y = yb[:, sl]
        y2 = y * y
        s_lo = jnp.sum(jnp.where(low, y2, 0.0), axis=-1, keepdims=True)
        s_hi = jnp.sum(jnp.where(low, 0.0, y2), axis=-1, keepdims=True)
        msq = jnp.where(low, s_lo, s_hi) * (1.0 / HEAD_DIM)
        yn = y * lax.rsqrt(msq + EPS) * cg_ref[:, sl]
        yc_ref[0, :, sl] = (yn * _silu(zc[:, sl])).astype(_BF16)

    qvt = _dot_nt(wqv_ref[...], u)
    qt_ref[0] = (qvt[:D_ATTN] * (LOG2E * HEAD_DIM ** -0.5)).astype(_BF16)
    vt_ref[0] = qvt[D_ATTN:].astype(_BF16)
    k_ref[0] = proj(0).astype(_BF16)
    z_ref[0] = proj(1).astype(_BF16)


def _inproj(x3, g, w_main, wqv_t, wf, bf, cw, cg, halo0, carry0, *, tm, emit_tail):
    b, l, d = x3.shape
    nt = l // tm
    kern = functools.partial(_inproj_kernel, tm=tm, emit_tail=emit_tail)
    const = lambda bi, i: (0, 0)
    row_blk = lambda bi, i: (bi, i, 0)
    col_blk = lambda bi, i: (bi, 0, i)
    rows = (jax.ShapeDtypeStruct((b, l, 512), _BF16), pl.BlockSpec((1, tm, 512), row_blk))
    cols = (jax.ShapeDtypeStruct((b, 512, l), _BF16), pl.BlockSpec((1, 512, tm), col_blk))
    out_shape, out_specs = (list(t) for t in zip(cols, rows, cols, rows, rows))
    out_shape.append(jax.ShapeDtypeStruct((b, l, LANES), _BF16))
    out_specs.append(pl.BlockSpec((1, tm, LANES), row_blk))
    if emit_tail:
        out_shape.append(jax.ShapeDtypeStruct((b, l, D_CONV), _F32))
        out_specs.append(pl.BlockSpec((1, tm, D_CONV), row_blk))
        out_shape.append(jax.ShapeDtypeStruct((b, l, LANES), _F32))
        out_specs.append(pl.BlockSpec((1, tm, LANES), row_blk))
    return pl.pallas_call(
        kern,
        grid=(b, nt),
        in_specs=[
            pl.BlockSpec((1, tm, d), row_blk),
            pl.BlockSpec((1, d), const),
            pl.BlockSpec(w_main.shape, const),
            pl.BlockSpec(wqv_t.shape, const),
            pl.BlockSpec(wf.shape, const),
            pl.BlockSpec((1, LANES), const),
            pl.BlockSpec((CONV_WIDTH, D_CONV), const),
            pl.BlockSpec((1, D_CONV), const),
            pl.BlockSpec((SUBLANES, D_CONV), const),
            pl.BlockSpec((1, LANES), const),
        ],
        out_specs=out_specs,
        out_shape=out_shape,
        scratch_shapes=[
            pltpu.VMEM((SUBLANES, D_CONV), _F32),
            pltpu.VMEM((1, LANES), _F32),
            pltpu.VMEM((tm + SUBLANES, D_CONV), _F32),
        ],
        compiler_params=pltpu.CompilerParams(
            dimension_semantics=("arbitrary", "arbitrary"),
            vmem_limit_bytes=VMEM_LIMIT),
        name="inproj_meta" if emit_tail else "inproj",
    )(x3, g, w_main, wqv_t, wf, bf, cw, cg, halo0, carry0)


def _attn_kernel(pq_ref, pk_ref, qt_ref, k_ref, aug_ref, vt_ref, km_ref, augm_ref, vmt_ref,
                 z_ref, g_ref, o_ref, q2t_s, sd_buf, su_buf, mbd_s, mbu_s, mask_s, m_s, acc_s,
                 *, tq, n_q, n_pairs):
    tk = tq
    pair = pl.program_id(1)
    sub = lax.broadcasted_iota(jnp.int32, (LANES, 1), 0)
    top = sub < HEAD_DIM
    ones_v = jnp.ones((ONES_ROWS, tk), _BF16)
    ones_m = jnp.ones((ONES_ROWS, N_META), _BF16)

    @pl.when(jnp.logical_and(pl.program_id(0) == 0, pair == 0))
    def _():
        c = lax.broadcasted_iota(jnp.int32, (tk, 2 * tq), 1)
        q_loc = jnp.where(c >= tq, c - tq, c)
        k_loc = lax.broadcasted_iota(jnp.int32, (tk, 2 * tq), 0)
        mask_s[0:tk, :] = jnp.where(k_loc <= q_loc, 0.0, -jnp.inf)
        mask_s[tk:, :] = jnp.zeros((N_META, 2 * tq), _F32)

    def ones_for(h):
        sel = jnp.logical_and(sub % ATTN_HEADS == h, sub < N_PIECES * ATTN_HEADS)
        return jnp.broadcast_to(jnp.where(sel, 1.0, 0.0).astype(_BF16), (LANES, tq))

    def build_q(qi, carry):
        qt = qt_ref[0, :, pl.ds(pl.multiple_of(qi * tq, tq), tq)]
        zero = jnp.zeros_like(qt)
        q2t_s[qi, 0:LANES, 0:tq] = jnp.where(top, qt, zero)
        q2t_s[qi, 0:LANES, tq:] = jnp.where(top, zero, qt)
        q2t_s[qi, LANES:, 0:tq] = ones_for(2 * pair)
        q2t_s[qi, LANES:, tq:] = ones_for(2 * pair + 1)
        return carry

    lax.fori_loop(0, n_q, build_q, 0)

    def keys(j):
        start = pl.multiple_of(j * tk, tk)
        return jnp.concatenate([k_ref[0, pl.ds(start, tk), :],
                                aug_ref[0, pl.ds(start, tk), :]], axis=1)

    def with_ones(vt, ones):
        return (jnp.concatenate([vt[0:HEAD_DIM], ones], axis=0),
                jnp.concatenate([vt[HEAD_DIM:], ones], axis=0))

    def values(j):
        start = pl.multiple_of(j * tk, tk)
        return with_ones(vt_ref[0, :, pl.ds(start, tk)], ones_v)

    def pv(vab, p):
        return jnp.concatenate(
            [jnp.dot(vab[0], p[:, :tq], preferred_element_type=_F32),
             jnp.dot(vab[1], p[:, tq:], preferred_element_type=_F32)], axis=1)

    def produce_diag(qi, slot):
        kx = jnp.concatenate(
            [keys(qi), jnp.concatenate([km_ref[...], augm_ref[...]], axis=1)], axis=0)
        s = jnp.dot(kx, q2t_s[qi], preferred_element_type=_F32) + mask_s[...]
        sd_buf[slot] = s
        mbd_s[slot] = jnp.max(s, axis=0, keepdims=True)

    def consume_diag(qi, slot):
        m = mbd_s[slot]
        p = jnp.exp2(sd_buf[slot] - m).astype(_BF16)
        m_s[qi] = m
        acc_s[qi] = pv(values(qi), p[0:tk]) + pv(with_ones(vmt_ref[...], ones_m), p[tk:])

    def produce(t, slot):
        s = jnp.dot(keys(pk_ref[t]), q2t_s[pq_ref[t]], preferred_element_type=_F32)
        su_buf[slot] = s
        mbu_s[slot] = jnp.max(s, axis=0, keepdims=True)

    def consume(t, slot):
        qi = pq_ref[t]
        m = m_s[qi]
        m_new = jnp.maximum(m, mbu_s[slot])
        alpha = jnp.exp2(m - m_new)
        p = jnp.exp2(su_buf[slot] - m_new).astype(_BF16)
        m_s[qi] = m_new
        acc_s[qi] = alpha * acc_s[qi] + pv(values(pk_ref[t]), p)

    def finalize(qi, carry):
        rows = pl.ds(pl.multiple_of(qi * tq, tq), tq)
        acc = acc_s[qi]
        ot = acc[0:HEAD_DIM] * (1.0 / acc[HEAD_DIM:HEAD_DIM + 1])
        oc = jnp.concatenate([ot[:, :tq], ot[:, tq:]], axis=0)
        o2 = oc * oc
        msa = jnp.sum(jnp.where(top, o2, 0.0), axis=0, keepdims=True)
        msb = jnp.sum(jnp.where(top, 0.0, o2), axis=0, keepdims=True)
        inv = jnp.where(top, lax.rsqrt(msa * (1.0 / HEAD_DIM) + EPS),
                        lax.rsqrt(msb * (1.0 / HEAD_DIM) + EPS))
        y = (oc * inv).T * g_ref[...]
        z = z_ref[0, rows, :].astype(_F32)
        o_ref[0, rows, :] = (y * _silu(z)).astype(_BF16)
        return carry

    produce_diag(0, 0)
    for qi in range(1, n_q):
        produce_diag(qi, qi % 2)
        consume_diag(qi - 1, (qi - 1) % 2)

    produce(0, 0)
    consume_diag(n_q - 1, (n_q - 1) % 2)

    def ticks(t0, n):
        for d in range(n):
            produce(t0 + d, (1 + d) % 2)
            consume(t0 + d - 1, d % 2)

    def tick_group(u, carry):
        ticks(TICKS_PER_TRIP * u + 1, TICKS_PER_TRIP)
        return carry

    n_trips = (n_pairs - 1) // TICKS_PER_TRIP
    lax.fori_loop(0, n_trips, tick_group, 0)
    ticks(TICKS_PER_TRIP * n_trips + 1, (n_pairs - 1) % TICKS_PER_TRIP)
    consume(n_pairs - 1, (n_pairs - 1) % 2)

    lax.fori_loop(0, n_q, finalize, 0)


def _attention(qt, k, aug, vt, km, augm, vmt, z, g, *, tq):
    b, l, _ = k.shape
    n_pair = D_ATTN // LANES
    n_q = l // tq
    assert n_q % 2 == 0 and n_q >= 2
    pq = [qi for qi in range(n_q) for _ in range(qi)]
    pk = [j for qi in range(n_q) for j in range(qi)]
    n_pairs = len(pq)
    assert n_pairs % 2 == 0 and n_pairs >= 2
    kern = functools.partial(_attn_kernel, tq=tq, n_q=n_q, n_pairs=n_pairs)
    seq_rows = pl.BlockSpec((1, l, LANES), lambda bi, p, *_: (bi, 0, p))
    seq_cols = pl.BlockSpec((1, LANES, l), lambda bi, p, *_: (bi, p, 0))
    rows_d = tq + N_META
    grid_spec = pltpu.PrefetchScalarGridSpec(
        num_scalar_prefetch=2,
        grid=(b, n_pair),
        in_specs=[
            seq_cols,
            seq_rows,
            pl.BlockSpec((1, l, LANES), lambda bi, p, *_: (bi, 0, 0)),
            seq_cols,
            pl.BlockSpec((N_META, LANES), lambda bi, p, *_: (0, p)),
            pl.BlockSpec((N_META, LANES), lambda bi, p, *_: (0, 0)),
            pl.BlockSpec((LANES, N_META), lambda bi, p, *_: (p, 0)),
            seq_rows,
            pl.BlockSpec((1, LANES), lambda bi, p, *_: (0, p)),
        ],
        out_specs=seq_rows,
        scratch_shapes=[
            pltpu.VMEM((n_q, 2 * LANES, 2 * tq), _BF16),
            pltpu.VMEM((2, rows_d, 2 * tq), _F32),
            pltpu.VMEM((2, tq, 2 * tq), _F32),
            pltpu.VMEM((2, 1, 2 * tq), _F32),
            pltpu.VMEM((2, 1, 2 * tq), _F32),
            pltpu.VMEM((rows_d, 2 * tq), _F32),
            pltpu.VMEM((n_q, 1, 2 * tq), _F32),
            pltpu.VMEM((n_q, HEAD_DIM + ONES_ROWS, 2 * tq), _F32),
        ],
    )
    return pl.pallas_call(
        kern,
        grid_spec=grid_spec,
        out_shape=jax.ShapeDtypeStruct((b, l, D_ATTN), _BF16),
        compiler_params=pltpu.CompilerParams(
            dimension_semantics=("arbitrary", "arbitrary"),
            vmem_limit_bytes=VMEM_LIMIT),
        name="fox_attention",
    )(jnp.asarray(pq, jnp.int32), jnp.asarray(pk, jnp.int32),
      qt, k, aug, vt, km, augm, vmt, z, g)


def _outproj_kernel(x_ref, ya_ref, yc_ref, w_ref, g_ref, o_ref):
    hres = (x_ref[0]
            + jnp.dot(ya_ref[0], w_ref[0:D_ATTN, :], preferred_element_type=_F32)
            + jnp.dot(yc_ref[0], w_ref[D_ATTN:, :], preferred_element_type=_F32))
    ms = jnp.mean(hres * hres, axis=-1, keepdims=True)
    o_ref[0] = hres * lax.rsqrt(ms + EPS) * g_ref[...]


def _outproj(x, ya, yc, w_out, g, *, tm):
    b, l, d = x.shape
    row_blk = lambda bi, i: (bi, i, 0)
    const = lambda bi, i: (0, 0)
    return pl.pallas_call(
        _outproj_kernel,
        grid=(b, l // tm),
        in_specs=[
            pl.BlockSpec((1, tm, d), row_blk),
            pl.BlockSpec((1, tm, D_ATTN), row_blk),
            pl.BlockSpec((1, tm, D_CONV), row_blk),
            pl.BlockSpec(w_out.shape, const),
            pl.BlockSpec((1, d), const),
        ],
        out_specs=pl.BlockSpec((1, tm, d), row_blk),
        out_shape=jax.ShapeDtypeStruct((b, l, d), _F32),
        compiler_params=pltpu.CompilerParams(
            dimension_semantics=("arbitrary", "arbitrary"),
            vmem_limit_bytes=VMEM_LIMIT),
        name="outproj",
    )(x, ya, yc, w_out, g)


def kernel(x, meta, norm_g, w_in, b_f, conv_w, attn_norm_g, conv_norm_g, w_out, final_norm_g):
    assert norm_g.shape[0] == 1, "single layer only"
    b, seq, d = x.shape
    w = w_in[0]
    f0 = 3 * D_ATTN
    n_rep = N_PIECES * ATTN_HEADS
    w_main = jnp.concatenate([w[:, D_ATTN:2 * D_ATTN], w[:, f0 + ATTN_HEADS:]],
                             axis=1).astype(_BF16)
    wqv_t = jnp.concatenate([w[:, :D_ATTN], w[:, 2 * D_ATTN:f0]], axis=1).T.astype(_BF16)
    wf8 = w[:, f0:f0 + ATTN_HEADS]
    wf = jnp.concatenate([wf8] * N_PIECES + [jnp.zeros((d, LANES - n_rep), w.dtype)],
                         axis=1).astype(_BF16)
    bf = jnp.concatenate([b_f[0]] * N_PIECES
                         + [jnp.zeros((LANES - n_rep,), b_f.dtype)])[None, :]
    cw = conv_w[0]

    meta_rows = LANES
    meta_p = jnp.pad(meta.astype(x.dtype), ((0, meta_rows - N_META), (0, 0)))[None]
    zero_halo = jnp.zeros((SUBLANES, D_CONV), _F32)
    zero_carry = jnp.zeros((1, LANES), _F32)
    _, km, vmt, _, _, augm, cxm, crow = _inproj(
        meta_p, norm_g, w_main, wqv_t, wf, bf, cw, conv_norm_g, zero_halo, zero_carry,
        tm=meta_rows, emit_tail=True)
    km = km[0, :N_META]
    vmt = vmt[0, :, :N_META]
    augm = augm[0, :N_META]
    halo0 = cxm[0, N_META - SUBLANES:N_META]
    carry0 = crow[0, N_META - 1:N_META]

    qt, k, vt, z, yc, aug = _inproj(x, norm_g, w_main, wqv_t, wf, bf, cw, conv_norm_g,
                                    halo0, carry0, tm=512, emit_tail=False)
    ya = _attention(qt, k, aug, vt, km, augm, vmt, z, attn_norm_g, tq=512)
    return _outproj(x, ya, yc, w_out[0].astype(_BF16), final_norm_g[None, :], tm=1024)
```

```python
import functools
import math

import jax
import jax.numpy as jnp
from jax import lax
from jax.experimental import pallas as pl
from jax.experimental.pallas import tpu as pltpu

D_MODEL = 1024
N_META = 16
ATTN_HEADS = 8
HEAD_DIM = 64
D_ATTN = 512
D_CONV = 512
CONV_WIDTH = 3
EPS = 1e-6
LANES = 128
SUBLANES = 8
N_PIECES = 3
ONES_ROWS = 16
TICKS_PER_TRIP = 4
LOG2E = math.log2(math.e)
VMEM_LIMIT = 48 * 1024 * 1024

_BF16 = jnp.bfloat16
_F32 = jnp.float32


def _log_sigmoid(x):
    return jnp.minimum(x, 0.0) - jnp.log1p(jnp.exp(-jnp.abs(x)))


def _silu(x):
    return x * (1.0 / (1.0 + jnp.exp(-x)))


def _dot_nt(a, b):
    return lax.dot_general(a, b, (((1,), (1,)), ((), ())),
                           preferred_element_type=_F32)


def _bf16_pieces(x, grp):
    hi = x.astype(_BF16).astype(_F32)
    r1 = x - hi
    mid = r1.astype(_BF16).astype(_F32)
    lo = r1 - mid
    return jnp.where(grp == 0, hi, jnp.where(grp == 1, mid,
                     jnp.where(grp == 2, lo, 0.0))).astype(_BF16)


def _inproj_kernel(x_ref, g_ref, w_ref, wqv_ref, wf_ref, bf_ref, cw_ref, cg_ref, halo0_ref,
                   carry0_ref, qt_ref, kx_ref, vt_ref, z_ref, yc_ref, *rest,
                   tm, emit_tail):
    if emit_tail:
        cx_ref, crow_ref, halo_s, carry_s, cx_s = rest
    else:
        halo_s, carry_s, cx_s = rest
    i = pl.program_id(1)

    @pl.when(i == 0)
    def _():
        halo_s[...] = halo0_ref[...]
        carry_s[...] = carry0_ref[...]

    x = x_ref[0]
    ms = jnp.mean(x * x, axis=-1, keepdims=True)
    u = (x * lax.rsqrt(ms + EPS) * g_ref[...]).astype(_BF16)

    def proj(c):
        return jnp.dot(u, w_ref[:, c * 512:(c + 1) * 512],
                       preferred_element_type=_F32)

    fl = jnp.dot(u, wf_ref[...], preferred_element_type=_F32)
    logf = _log_sigmoid(fl + bf_ref[...])
    lane = lax.broadcasted_iota(jnp.int32, (1, LANES), 1)
    grp = lane // ATTN_HEADS
    row = lax.broadcasted_iota(jnp.int32, (tm, tm), 0)
    col = lax.broadcasted_iota(jnp.int32, (tm, tm), 1)
    tri = jnp.where(col <= row, 1.0, 0.0).astype(_BF16)
    c3 = jnp.dot(tri, _bf16_pieces(logf, grp), preferred_element_type=_F32)
    h8 = ATTN_HEADS
    csum = (c3 + pltpu.roll(c3, h8, axis=1) + pltpu.roll(c3, 2 * h8, axis=1)
            + pltpu.roll(c3, LANES - h8, axis=1) + pltpu.roll(c3, LANES - 2 * h8, axis=1))
    csum = csum + carry_s[...]
    carry_s[...] = csum[tm - 1:tm, :]
    aug = pltpu.roll(_bf16_pieces(csum * (-LOG2E), grp).astype(_F32), HEAD_DIM, axis=1)
    if emit_tail:
        crow_ref[0] = csum

    gate_b = proj(2)
    cx = proj(3) * proj(4)
    cx_s[0:SUBLANES, :] = halo_s[...]
    cx_s[SUBLANES:SUBLANES + tm, :] = cx
    if emit_tail:
        cx_ref[0] = cx
    conv = (cw_ref[0:1, :] * cx_s[SUBLANES - 2:SUBLANES - 2 + tm, :]
            + cw_ref[1:2, :] * cx_s[SUBLANES - 1:SUBLANES - 1 + tm, :]
            + cw_ref[2:3, :] * cx)
    halo_s[...] = cx[tm - SUBLANES:tm, :]
    yb = gate_b * conv
    zc = proj(5)
    low = lane < HEAD_DIM
    for cb in range(D_CONV // LANES):
        sl = slice(cb * LANES, (cb + 1) * LANES)
        y = yb[:, sl]
        y2 = y * y
        s_lo = jnp.sum(jnp.where(low, y2, 0.0), axis=-1, keepdims=True)
        s_hi = jnp.sum(jnp.where(low, 0.0, y2), axis=-1, keepdims=True)
        msq = jnp.where(low, s_lo, s_hi) * (1.0 / HEAD_DIM)
        yn = y * lax.rsqrt(msq + EPS) * cg_ref[:, sl]
        yc_ref[0, :, sl] = (yn * _silu(zc[:, sl])).astype(_BF16)

    qvt = _dot_nt(wqv_ref[...], u)
    qt_ref[0] = (qvt[:D_ATTN] * (LOG2E * HEAD_DIM ** -0.5)).astype(_BF16)
    vt_ref[0] = qvt[D_ATTN:].astype(_BF16)
    k = proj(0)
    for p in range(D_ATTN // LANES):
        kp = k[:, p * LANES:(p + 1) * LANES]
        sl = slice(2 * p * LANES, (2 * p + 1) * LANES)
        kx_ref[0, :, sl] = jnp.where(low, kp, aug).astype(_BF16)
        sl = slice((2 * p + 1) * LANES, (2 * p + 2) * LANES)
        kx_ref[0, :, sl] = jnp.where(low, pltpu.roll(kp, HEAD_DIM, axis=1), aug).astype(_BF16)
    z_ref[0] = proj(1).astype(_BF16)


def _inproj(x3, g, w_main, wqv_t, wf, bf, cw, cg, halo0, carry0, *, tm, emit_tail):
    b, l, d = x3.shape
    nt = l // tm
    kern = functools.partial(_inproj_kernel, tm=tm, emit_tail=emit_tail)
    const = lambda bi, i: (0, 0)
    row_blk = lambda bi, i: (bi, i, 0)
    col_blk = lambda bi, i: (bi, 0, i)
    rows = (jax.ShapeDtypeStruct((b, l, 512), _BF16), pl.BlockSpec((1, tm, 512), row_blk))
    cols = (jax.ShapeDtypeStruct((b, 512, l), _BF16), pl.BlockSpec((1, 512, tm), col_blk))
    keys = (jax.ShapeDtypeStruct((b, l, ATTN_HEADS * LANES), _BF16),
            pl.BlockSpec((1, tm, ATTN_HEADS * LANES), row_blk))
    out_shape, out_specs = (list(t) for t in zip(cols, keys, cols, rows, rows))
    if emit_tail:
        out_shape.append(jax.ShapeDtypeStruct((b, l, D_CONV), _F32))
        out_specs.append(pl.BlockSpec((1, tm, D_CONV), row_blk))
        out_shape.append(jax.ShapeDtypeStruct((b, l, LANES), _F32))
        out_specs.append(pl.BlockSpec((1, tm, LANES), row_blk))
    return pl.pallas_call(
        kern,
        grid=(b, nt),
        in_specs=[
            pl.BlockSpec((1, tm, d), row_blk),
            pl.BlockSpec((1, d), const),
            pl.BlockSpec(w_main.shape, const),
            pl.BlockSpec(wqv_t.shape, const),
            pl.BlockSpec(wf.shape, const),
            pl.BlockSpec((1, LANES), const),
            pl.BlockSpec((CONV_WIDTH, D_CONV), const),
            pl.BlockSpec((1, D_CONV), const),
            pl.BlockSpec((SUBLANES, D_CONV), const),
            pl.BlockSpec((1, LANES), const),
        ],
        out_specs=out_specs,
        out_shape=out_shape,
        scratch_shapes=[
            pltpu.VMEM((SUBLANES, D_CONV), _F32),
            pltpu.VMEM((1, LANES), _F32),
            pltpu.VMEM((tm + SUBLANES, D_CONV), _F32),
        ],
        compiler_params=pltpu.CompilerParams(
            dimension_semantics=("arbitrary", "arbitrary"),
            vmem_limit_bytes=VMEM_LIMIT),
        name="inproj_meta" if emit_tail else "inproj",
    )(x3, g, w_main, wqv_t, wf, bf, cw, cg, halo0, carry0)


def _attn_kernel(pq_ref, pk_ref, qt_ref, kx_ref, vt_ref, kxm_ref, vmt_ref,
                 z_ref, g_ref, o_ref, q2t_s, sd_buf, su_buf, mbd_s, mbu_s, mask_s, m_s, acc_s,
                 *, tq, n_q, n_pairs):
    tk = tq
    pair = pl.program_id(1)
    sub = lax.broadcasted_iota(jnp.int32, (LANES, 1), 0)
    top = sub < HEAD_DIM
    ones_v = jnp.ones((ONES_ROWS, tk), _BF16)
    ones_m = jnp.ones((ONES_ROWS, N_META), _BF16)

    @pl.when(jnp.logical_and(pl.program_id(0) == 0, pair == 0))
    def _():
        c = lax.broadcasted_iota(jnp.int32, (tk, 2 * tq), 1)
        q_loc = jnp.where(c >= tq, c - tq, c)
        k_loc = lax.broadcasted_iota(jnp.int32, (tk, 2 * tq), 0)
        mask_s[0:tk, :] = jnp.where(k_loc <= q_loc, 0.0, -jnp.inf)
        mask_s[tk:, :] = jnp.zeros((N_META, 2 * tq), _F32)

    def ones_for(h):
        r = lax.broadcasted_iota(jnp.int32, (HEAD_DIM, 1), 0)
        sel = jnp.logical_and(r % ATTN_HEADS == h, r < N_PIECES * ATTN_HEADS)
        return jnp.broadcast_to(jnp.where(sel, 1.0, 0.0).astype(_BF16), (HEAD_DIM, tq))

    def build_q(qi, carry):
        qt = qt_ref[0, :, pl.ds(pl.multiple_of(qi * tq, tq), tq)]
        q2t_s[qi, 0:HEAD_DIM, 0:tq] = qt[0:HEAD_DIM]
        q2t_s[qi, 0:HEAD_DIM, tq:] = qt[HEAD_DIM:]
        q2t_s[qi, HEAD_DIM:, 0:tq] = ones_for(2 * pair)
        q2t_s[qi, HEAD_DIM:, tq:] = ones_for(2 * pair + 1)
        return carry

    lax.fori_loop(0, n_q, build_q, 0)

    def keys(j):
        return kx_ref[0, pl.ds(pl.multiple_of(j * tk, tk), tk), :]

    def logits(kx, q2t):
        return jnp.concatenate(
            [jnp.dot(kx[:, :LANES], q2t[:, :tq], preferred_element_type=_F32),
             jnp.dot(kx[:, LANES:], q2t[:, tq:], preferred_element_type=_F32)], axis=1)

    def values(j):
        start = pl.multiple_of(j * tk, tk)
        return jnp.concatenate([vt_ref[0, :, pl.ds(start, tk)], ones_v], axis=0)

    def produce_diag(qi, slot):
        kx = jnp.concatenate([keys(qi), kxm_ref[...]], axis=0)
        s = logits(kx, q2t_s[qi]) + mask_s[...]
        sd_buf[slot] = s
        mbd_s[slot] = jnp.max(s, axis=0, keepdims=True)

    def consume_diag(qi, slot):
        m = mbd_s[slot]
        p = jnp.exp2(sd_buf[slot] - m).astype(_BF16)
        m_s[qi] = m
        acc_s[qi] = (jnp.dot(values(qi), p[0:tk], preferred_element_type=_F32)
                     + jnp.dot(jnp.concatenate([vmt_ref[...], ones_m], axis=0), p[tk:],
                               preferred_element_type=_F32))

    def produce(t, slot):
        s = logits(keys(pk_ref[t]), q2t_s[pq_ref[t]])
        su_buf[slot] = s
        mbu_s[slot] = jnp.max(s, axis=0, keepdims=True)

    def consume(t, slot):
        qi = pq_ref[t]
        m = m_s[qi]
        m_new = jnp.maximum(m, mbu_s[slot])
        alpha = jnp.exp2(m - m_new)
        p = jnp.exp2(su_buf[slot] - m_new).astype(_BF16)
        m_s[qi] = m_new
        acc_s[qi] = alpha * acc_s[qi] + jnp.dot(values(pk_ref[t]), p,
                                                preferred_element_type=_F32)

    def finalize(qi, carry):
        rows = pl.ds(pl.multiple_of(qi * tq, tq), tq)
        acc = acc_s[qi]
        ot = acc[0:LANES] * (1.0 / acc[LANES:LANES + 1])
        oc = jnp.where(top, ot[:, :tq], ot[:, tq:])
        o2 = oc * oc
        msa = jnp.sum(jnp.where(top, o2, 0.0), axis=0, keepdims=True)
        msb = jnp.sum(jnp.where(top, 0.0, o2), axis=0, keepdims=True)
        inv = jnp.where(top, lax.rsqrt(msa * (1.0 / HEAD_DIM) + EPS),
                        lax.rsqrt(msb * (1.0 / HEAD_DIM) + EPS))
        y = (oc * inv).T * g_ref[...]
        z = z_ref[0, rows, :].astype(_F32)
        o_ref[0, rows, :] = (y * _silu(z)).astype(_BF16)
        return carry

    produce_diag(0, 0)
    for qi in range(1, n_q):
        produce_diag(qi, qi % 2)
        consume_diag(qi - 1, (qi - 1) % 2)

    produce(0, 0)
    consume_diag(n_q - 1, (n_q - 1) % 2)

    def ticks(t0, n):
        for d in range(n):
            produce(t0 + d, (1 + d) % 2)
            consume(t0 + d - 1, d % 2)

    def tick_group(u, carry):
        ticks(TICKS_PER_TRIP * u + 1, TICKS_PER_TRIP)
        return carry

    n_trips = (n_pairs - 1) // TICKS_PER_TRIP
    lax.fori_loop(0, n_trips, tick_group, 0)
    ticks(TICKS_PER_TRIP * n_trips + 1, (n_pairs - 1) % TICKS_PER_TRIP)
    consume(n_pairs - 1, (n_pairs - 1) % 2)

    lax.fori_loop(0, n_q, finalize, 0)


def _attention(qt, kx, vt, kxm, vmt, z, g, *, tq):
    b, l, _ = z.shape
    n_pair = D_ATTN // LANES
    n_q = l // tq
    assert n_q % 2 == 0 and n_q >= 2
    pq = [qi for qi in range(n_q) for _ in range(qi)]
    pk = [j for qi in range(n_q) for j in range(qi)]
    n_pairs = len(pq)
    assert n_pairs % 2 == 0 and n_pairs >= 2
    kern = functools.partial(_attn_kernel, tq=tq, n_q=n_q, n_pairs=n_pairs)
    seq_rows = pl.BlockSpec((1, l, LANES), lambda bi, p, *_: (bi, 0, p))
    seq_cols = pl.BlockSpec((1, LANES, l), lambda bi, p, *_: (bi, p, 0))
    rows_d = tq + N_META
    grid_spec = pltpu.PrefetchScalarGridSpec(
        num_scalar_prefetch=2,
        grid=(b, n_pair),
        in_specs=[
            seq_cols,
            pl.BlockSpec((1, l, 2 * LANES), lambda bi, p, *_: (bi, 0, p)),
            seq_cols,
            pl.BlockSpec((N_META, 2 * LANES), lambda bi, p, *_: (0, p)),
            pl.BlockSpec((LANES, N_META), lambda bi, p, *_: (p, 0)),
            seq_rows,
            pl.BlockSpec((1, LANES), lambda bi, p, *_: (0, p)),
        ],
        out_specs=seq_rows,
        scratch_shapes=[
            pltpu.VMEM((n_q, LANES, 2 * tq), _BF16),
            pltpu.VMEM((2, rows_d, 2 * tq), _F32),
            pltpu.VMEM((2, tq, 2 * tq), _F32),
            pltpu.VMEM((2, 1, 2 * tq), _F32),
            pltpu.VMEM((2, 1, 2 * tq), _F32),
            pltpu.VMEM((rows_d, 2 * tq), _F32),
            pltpu.VMEM((n_q, 1, 2 * tq), _F32),
            pltpu.VMEM((n_q, LANES + ONES_ROWS, 2 * tq), _F32),
        ],
    )
    return pl.pallas_call(
        kern,
        grid_spec=grid_spec,
        out_shape=jax.ShapeDtypeStruct((b, l, D_ATTN), _BF16),
        compiler_params=pltpu.CompilerParams(
            dimension_semantics=("arbitrary", "arbitrary"),
            vmem_limit_bytes=VMEM_LIMIT),
        name="fox_attention",
    )(jnp.asarray(pq, jnp.int32), jnp.asarray(pk, jnp.int32),
      qt, kx, vt, kxm, vmt, z, g)


def _outproj_kernel(x_ref, ya_ref, yc_ref, w_ref, g_ref, o_ref):
    hres = (x_ref[0]
            + jnp.dot(ya_ref[0], w_ref[0:D_ATTN, :], preferred_element_type=_F32)
            + jnp.dot(yc_ref[0], w_ref[D_ATTN:, :], preferred_element_type=_F32))
    ms = jnp.mean(hres * hres, axis=-1, keepdims=True)
    o_ref[0] = hres * lax.rsqrt(ms + EPS) * g_ref[...]


def _outproj(x, ya, yc, w_out, g, *, tm):
    b, l, d = x.shape
    row_blk = lambda bi, i: (bi, i, 0)
    const = lambda bi, i: (0, 0)
    return pl.pallas_call(
        _outproj_kernel,
        grid=(b, l // tm),
        in_specs=[
            pl.BlockSpec((1, tm, d), row_blk),
            pl.BlockSpec((1, tm, D_ATTN), row_blk),
            pl.BlockSpec((1, tm, D_CONV), row_blk),
            pl.BlockSpec(w_out.shape, const),
            pl.BlockSpec((1, d), const),
        ],
        out_specs=pl.BlockSpec((1, tm, d), row_blk),
        out_shape=jax.ShapeDtypeStruct((b, l, d), _F32),
        compiler_params=pltpu.CompilerParams(
            dimension_semantics=("arbitrary", "arbitrary"),
            vmem_limit_bytes=VMEM_LIMIT),
        name="outproj",
    )(x, ya, yc, w_out, g)


def kernel(x, meta, norm_g, w_in, b_f, conv_w, attn_norm_g, conv_norm_g, w_out, final_norm_g):
    assert norm_g.shape[0] == 1, "single layer only"
    b, seq, d = x.shape
    w = w_in[0]
    f0 = 3 * D_ATTN
    n_rep = N_PIECES * ATTN_HEADS
    w_main = jnp.concatenate([w[:, D_ATTN:2 * D_ATTN], w[:, f0 + ATTN_HEADS:]],
                             axis=1).astype(_BF16)
    wqv_t = jnp.concatenate([w[:, :D_ATTN], w[:, 2 * D_ATTN:f0]], axis=1).T.astype(_BF16)
    wf8 = w[:, f0:f0 + ATTN_HEADS]
    wf = jnp.concatenate([wf8] * N_PIECES + [jnp.zeros((d, LANES - n_rep), w.dtype)],
                         axis=1).astype(_BF16)
    bf = jnp.concatenate([b_f[0]] * N_PIECES
                         + [jnp.zeros((LANES - n_rep,), b_f.dtype)])[None, :]
    cw = conv_w[0]

    meta_rows = LANES
    meta_p = jnp.pad(meta.astype(x.dtype), ((0, meta_rows - N_META), (0, 0)))[None]
    zero_halo = jnp.zeros((SUBLANES, D_CONV), _F32)
    zero_carry = jnp.zeros((1, LANES), _F32)
    _, kxm, vmt, _, _, cxm, crow = _inproj(
        meta_p, norm_g, w_main, wqv_t, wf, bf, cw, conv_norm_g, zero_halo, zero_carry,
        tm=meta_rows, emit_tail=True)
    kxm = kxm[0, :N_META]
    vmt = vmt[0, :, :N_META]
    halo0 = cxm[0, N_META - SUBLANES:N_META]
    carry0 = crow[0, N_META - 1:N_META]

    qt, kx, vt, z, yc = _inproj(x, norm_g, w_main, wqv_t, wf, bf, cw, conv_norm_g,
                                halo0, carry0, tm=512, emit_tail=False)
    ya = _attention(qt, kx, vt, kxm, vmt, z, attn_norm_g, tq=512)
    return _outproj(x, ya, yc, w_out[0].astype(_BF16), final_norm_g[None, :], tm=1024)
```

```python
import functools
import math

import jax
import jax.numpy as jnp
from jax import lax
from jax.experimental import pallas as pl
from jax.experimental.pallas import tpu as pltpu

D_MODEL = 1024
N_META = 16
ATTN_HEADS = 8
HEAD_DIM = 64
D_ATTN = 512
D_CONV = 512
CONV_WIDTH = 3
EPS = 1e-6
LANES = 128
SUBLANES = 8
N_PIECES = 3
ONES_ROWS = 16
TICKS_PER_TRIP = 4
LOG2E = math.log2(math.e)
VMEM_LIMIT = 48 * 1024 * 1024

_BF16 = jnp.bfloat16
_F32 = jnp.float32


def _log_sigmoid(x):
    return jnp.minimum(x, 0.0) - jnp.log1p(jnp.exp(-jnp.abs(x)))


def _silu(x):
    return x * (1.0 / (1.0 + jnp.exp(-x)))


def _dot_nt(a, b):
    return lax.dot_general(a, b, (((1,), (1,)), ((), ())),
                           preferred_element_type=_F32)


def _bf16_pieces(x):
    hi = x.astype(_BF16).astype(_F32)
    r1 = x - hi
    mid = r1.astype(_BF16).astype(_F32)
    return hi, mid, r1 - mid


def _inproj_kernel(x_ref, g_ref, w_ref, wqv_ref, bf_ref, cw_ref, cg_ref, halo0_ref,
                   carry0_ref, qt_ref, kx_ref, vt_ref, z_ref, yc_ref, *rest,
                   tm, emit_tail):
    if emit_tail:
        cx_ref, crow_ref, halo_s, carry_s, cx_s = rest
    else:
        halo_s, carry_s, cx_s = rest
    i = pl.program_id(1)

    @pl.when(i == 0)
    def _():
        halo_s[...] = halo0_ref[...]
        carry_s[...] = jnp.broadcast_to(carry0_ref[...], (ATTN_HEADS, LANES))

    x = x_ref[0]
    ms = jnp.mean(x * x, axis=-1, keepdims=True)
    u = (x * lax.rsqrt(ms + EPS) * g_ref[...]).astype(_BF16)

    def proj(c):
        return jnp.dot(u, w_ref[:, c * 512:(c + 1) * 512],
                       preferred_element_type=_F32)

    gate_b = proj(2)
    cx = proj(3) * proj(4)
    cx_s[0:SUBLANES, :] = halo_s[...]
    cx_s[SUBLANES:SUBLANES + tm, :] = cx
    if emit_tail:
        cx_ref[0] = cx
    conv = (cw_ref[0:1, :] * cx_s[SUBLANES - 2:SUBLANES - 2 + tm, :]
            + cw_ref[1:2, :] * cx_s[SUBLANES - 1:SUBLANES - 1 + tm, :]
            + cw_ref[2:3, :] * cx)
    halo_s[...] = cx[tm - SUBLANES:tm, :]
    yb = gate_b * conv
    zc = proj(5)
    low = lax.broadcasted_iota(jnp.int32, (1, LANES), 1) < HEAD_DIM
    for cb in range(D_CONV // LANES):
        sl = slice(cb * LANES, (cb + 1) * LANES)
        y = yb[:, sl]
        y2 = y * y
        s_lo = jnp.sum(jnp.where(low, y2, 0.0), axis=-1, keepdims=True)
        s_hi = jnp.sum(jnp.where(low, 0.0, y2), axis=-1, keepdims=True)
        msq = jnp.where(low, s_lo, s_hi) * (1.0 / HEAD_DIM)
        yn = y * lax.rsqrt(msq + EPS) * cg_ref[:, sl]
        yc_ref[0, :, sl] = (yn * _silu(zc[:, sl])).astype(_BF16)

    qvt = _dot_nt(wqv_ref[...], u)
    qt_ref[0] = (qvt[:D_ATTN] * (LOG2E * HEAD_DIM ** -0.5)).astype(_BF16)
    vt_ref[0] = qvt[D_ATTN:2 * D_ATTN].astype(_BF16)

    logf = _log_sigmoid(qvt[2 * D_ATTN:2 * D_ATTN + ATTN_HEADS] + bf_ref[...])
    hi, mid, lo = _bf16_pieces(logf)
    pieces = jnp.concatenate([hi, mid, lo, jnp.zeros_like(hi)], axis=0).astype(_BF16)
    row = lax.broadcasted_iota(jnp.int32, (tm, tm), 0)
    col = lax.broadcasted_iota(jnp.int32, (tm, tm), 1)
    tri = jnp.where(row <= col, 1.0, 0.0).astype(_BF16)
    c3 = jnp.dot(pieces, tri, preferred_element_type=_F32)
    h8 = ATTN_HEADS
    csum = c3[0:h8] + c3[h8:2 * h8] + c3[2 * h8:3 * h8] + carry_s[:, 0:1]
    carry_s[...] = jnp.broadcast_to(csum[:, tm - 1:tm], (h8, LANES))
    if emit_tail:
        crow_ref[0] = csum
    hi, mid, lo = _bf16_pieces(csum * (-LOG2E))
    aug = jnp.concatenate(
        [jnp.zeros((HEAD_DIM, tm), _F32), hi, mid, lo,
         jnp.zeros((LANES - HEAD_DIM - N_PIECES * h8, tm), _F32)], axis=0).T

    k = proj(0)
    for p in range(D_ATTN // LANES):
        kp = k[:, p * LANES:(p + 1) * LANES]
        sl = slice(2 * p * LANES, (2 * p + 1) * LANES)
        kx_ref[0, :, sl] = jnp.where(low, kp, aug).astype(_BF16)
        sl = slice((2 * p + 1) * LANES, (2 * p + 2) * LANES)
        kx_ref[0, :, sl] = jnp.where(low, pltpu.roll(kp, HEAD_DIM, axis=1), aug).astype(_BF16)
    z_ref[0] = proj(1).astype(_BF16)


def _inproj(x3, g, w_main, wqv_t, bf, cw, cg, halo0, carry0, *, tm, emit_tail):
    b, l, d = x3.shape
    nt = l // tm
    kern = functools.partial(_inproj_kernel, tm=tm, emit_tail=emit_tail)
    const = lambda bi, i: (0, 0)
    row_blk = lambda bi, i: (bi, i, 0)
    col_blk = lambda bi, i: (bi, 0, i)
    rows = (jax.ShapeDtypeStruct((b, l, 512), _BF16), pl.BlockSpec((1, tm, 512), row_blk))
    cols = (jax.ShapeDtypeStruct((b, 512, l), _BF16), pl.BlockSpec((1, 512, tm), col_blk))
    keys = (jax.ShapeDtypeStruct((b, l, ATTN_HEADS * LANES), _BF16),
            pl.BlockSpec((1, tm, ATTN_HEADS * LANES), row_blk))
    out_shape, out_specs = (list(t) for t in zip(cols, keys, cols, rows, rows))
    if emit_tail:
        out_shape.append(jax.ShapeDtypeStruct((b, l, D_CONV), _F32))
        out_specs.append(pl.BlockSpec((1, tm, D_CONV), row_blk))
        out_shape.append(jax.ShapeDtypeStruct((b, ATTN_HEADS, l), _F32))
        out_specs.append(pl.BlockSpec((1, ATTN_HEADS, tm), col_blk))
    return pl.pallas_call(
        kern,
        grid=(b, nt),
        in_specs=[
            pl.BlockSpec((1, tm, d), row_blk),
            pl.BlockSpec((1, d), const),
            pl.BlockSpec(w_main.shape, const),
            pl.BlockSpec(wqv_t.shape, const),
            pl.BlockSpec((ATTN_HEADS, 1), const),
            pl.BlockSpec((CONV_WIDTH, D_CONV), const),
            pl.BlockSpec((1, D_CONV), const),
            pl.BlockSpec((SUBLANES, D_CONV), const),
            pl.BlockSpec((ATTN_HEADS, 1), const),
        ],
        out_specs=out_specs,
        out_shape=out_shape,
        scratch_shapes=[
            pltpu.VMEM((SUBLANES, D_CONV), _F32),
            pltpu.VMEM((ATTN_HEADS, LANES), _F32),
            pltpu.VMEM((tm + SUBLANES, D_CONV), _F32),
        ],
        compiler_params=pltpu.CompilerParams(
            dimension_semantics=("arbitrary", "arbitrary"),
            vmem_limit_bytes=VMEM_LIMIT),
        name="inproj_meta" if emit_tail else "inproj",
    )(x3, g, w_main, wqv_t, bf, cw, cg, halo0, carry0)


def _attn_kernel(pq_ref, pk_ref, qt_ref, kx_ref, vt_ref, kxm_ref, vmt_ref,
                 z_ref, g_ref, o_ref, q2t_s, sd_buf, su_buf, mbd_s, mbu_s, mask_s, m_s, acc_s,
                 *, tq, n_q, n_pairs):
    tk = tq
    pair = pl.program_id(1)
    sub = lax.broadcasted_iota(jnp.int32, (LANES, 1), 0)
    top = sub < HEAD_DIM
    ones_v = jnp.ones((ONES_ROWS, tk), _BF16)
    ones_m = jnp.ones((ONES_ROWS, N_META), _BF16)

    @pl.when(jnp.logical_and(pl.program_id(0) == 0, pair == 0))
    def _():
        c = lax.broadcasted_iota(jnp.int32, (tk, 2 * tq), 1)
        q_loc = jnp.where(c >= tq, c - tq, c)
        k_loc = lax.broadcasted_iota(jnp.int32, (tk, 2 * tq), 0)
        mask_s[0:tk, :] = jnp.where(k_loc <= q_loc, 0.0, -jnp.inf)
        mask_s[tk:, :] = jnp.zeros((N_META, 2 * tq), _F32)

    def ones_for(h):
        r = lax.broadcasted_iota(jnp.int32, (HEAD_DIM, 1), 0)
        sel = jnp.logical_and(r % ATTN_HEADS == h, r < N_PIECES * ATTN_HEADS)
        return jnp.broadcast_to(jnp.where(sel, 1.0, 0.0).astype(_BF16), (HEAD_DIM, tq))

    def build_q(qi, carry):
        qt = qt_ref[0, :, pl.ds(pl.multiple_of(qi * tq, tq), tq)]
        q2t_s[qi, 0:HEAD_DIM, 0:tq] = qt[0:HEAD_DIM]
        q2t_s[qi, 0:HEAD_DIM, tq:] = qt[HEAD_DIM:]
        q2t_s[qi, HEAD_DIM:, 0:tq] = ones_for(2 * pair)
        q2t_s[qi, HEAD_DIM:, tq:] = ones_for(2 * pair + 1)
        return carry

    lax.fori_loop(0, n_q, build_q, 0)

    def keys(j):
        return kx_ref[0, pl.ds(pl.multiple_of(j * tk, tk), tk), :]

    def logits(kx, q2t):
        return jnp.concatenate(
            [jnp.dot(kx[:, :LANES], q2t[:, :tq], preferred_element_type=_F32),
             jnp.dot(kx[:, LANES:], q2t[:, tq:], preferred_element_type=_F32)], axis=1)

    def values(j):
        start = pl.multiple_of(j * tk, tk)
        return jnp.concatenate([vt_ref[0, :, pl.ds(start, tk)], ones_v], axis=0)

    def produce_diag(qi, slot):
        kx = jnp.concatenate([keys(qi), kxm_ref[...]], axis=0)
        s = logits(kx, q2t_s[qi]) + mask_s[...]
        sd_buf[slot] = s
        mbd_s[slot] = jnp.max(s, axis=0, keepdims=True)

    def consume_diag(qi, slot):
        m = mbd_s[slot]
        p = jnp.exp2(sd_buf[slot] - m).astype(_BF16)
        m_s[qi] = m
        acc_s[qi] = (jnp.dot(values(qi), p[0:tk], preferred_element_type=_F32)
                     + jnp.dot(jnp.concatenate([vmt_ref[...], ones_m], axis=0), p[tk:],
                               preferred_element_type=_F32))

    def produce(t, slot):
        s = logits(keys(pk_ref[t]), q2t_s[pq_ref[t]])
        su_buf[slot] = s
        mbu_s[slot] = jnp.max(s, axis=0, keepdims=True)

    def consume(t, slot):
        qi = pq_ref[t]
        m = m_s[qi]
        m_new = jnp.maximum(m, mbu_s[slot])
        alpha = jnp.exp2(m - m_new)
        p = jnp.exp2(su_buf[slot] - m_new).astype(_BF16)
        m_s[qi] = m_new
        acc_s[qi] = alpha * acc_s[qi] + jnp.dot(values(pk_ref[t]), p,
                                                preferred_element_type=_F32)

    def finalize(qi, carry):
        rows = pl.ds(pl.multiple_of(qi * tq, tq), tq)
        acc = acc_s[qi]
        ot = acc[0:LANES] * (1.0 / acc[LANES:LANES + 1])
        oc = jnp.where(top, ot[:, :tq], ot[:, tq:])
        o2 = oc * oc
        msa = jnp.sum(jnp.where(top, o2, 0.0), axis=0, keepdims=True)
        msb = jnp.sum(jnp.where(top, 0.0, o2), axis=0, keepdims=True)
        inv = jnp.where(top, lax.rsqrt(msa * (1.0 / HEAD_DIM) + EPS),
                        lax.rsqrt(msb * (1.0 / HEAD_DIM) + EPS))
        y = (oc * inv).T * g_ref[...]
        z = z_ref[0, rows, :].astype(_F32)
        o_ref[0, rows, :] = (y * _silu(z)).astype(_BF16)
        return carry

    produce_diag(0, 0)
    for qi in range(1, n_q):
        produce_diag(qi, qi % 2)
        consume_diag(qi - 1, (qi - 1) % 2)

    produce(0, 0)
    consume_diag(n_q - 1, (n_q - 1) % 2)

    def ticks(t0, n):
        for d in range(n):
            produce(t0 + d, (1 + d) % 2)
            consume(t0 + d - 1, d % 2)

    def tick_group(u, carry):
        ticks(TICKS_PER_TRIP * u + 1, TICKS_PER_TRIP)
        return carry

    n_trips = (n_pairs - 1) // TICKS_PER_TRIP
    lax.fori_loop(0, n_trips, tick_group, 0)
    ticks(TICKS_PER_TRIP * n_trips + 1, (n_pairs - 1) % TICKS_PER_TRIP)
    consume(n_pairs - 1, (n_pairs - 1) % 2)

    lax.fori_loop(0, n_q, finalize, 0)


def _attention(qt, kx, vt, kxm, vmt, z, g, *, tq):
    b, l, _ = z.shape
    n_pair = D_ATTN // LANES
    n_q = l // tq
    assert n_q % 2 == 0 and n_q >= 2
    pq = [qi for qi in range(n_q) for _ in range(qi)]
    pk = [j for qi in range(n_q) for j in range(qi)]
    n_pairs = len(pq)
    assert n_pairs % 2 == 0 and n_pairs >= 2
    kern = functools.partial(_attn_kernel, tq=tq, n_q=n_q, n_pairs=n_pairs)
    seq_rows = pl.BlockSpec((1, l, LANES), lambda bi, p, *_: (bi, 0, p))
    seq_cols = pl.BlockSpec((1, LANES, l), lambda bi, p, *_: (bi, p, 0))
    rows_d = tq + N_META
    grid_spec = pltpu.PrefetchScalarGridSpec(
        num_scalar_prefetch=2,
        grid=(b, n_pair),
        in_specs=[
            seq_cols,
            pl.BlockSpec((1, l, 2 * LANES), lambda bi, p, *_: (bi, 0, p)),
            seq_cols,
            pl.BlockSpec((N_META, 2 * LANES), lambda bi, p, *_: (0, p)),
            pl.BlockSpec((LANES, N_META), lambda bi, p, *_: (p, 0)),
            seq_rows,
            pl.BlockSpec((1, LANES), lambda bi, p, *_: (0, p)),
        ],
        out_specs=seq_rows,
        scratch_shapes=[
            pltpu.VMEM((n_q, LANES, 2 * tq), _BF16),
            pltpu.VMEM((2, rows_d, 2 * tq), _F32),
            pltpu.VMEM((2, tq, 2 * tq), _F32),
            pltpu.VMEM((2, 1, 2 * tq), _F32),
            pltpu.VMEM((2, 1, 2 * tq), _F32),
            pltpu.VMEM((rows_d, 2 * tq), _F32),
            pltpu.VMEM((n_q, 1, 2 * tq), _F32),
            pltpu.VMEM((n_q, LANES + ONES_ROWS, 2 * tq), _F32),
        ],
    )
    return pl.pallas_call(
        kern,
        grid_spec=grid_spec,
        out_shape=jax.ShapeDtypeStruct((b, l, D_ATTN), _BF16),
        compiler_params=pltpu.CompilerParams(
            dimension_semantics=("arbitrary", "arbitrary"),
            vmem_limit_bytes=VMEM_LIMIT),
        name="fox_attention",
    )(jnp.asarray(pq, jnp.int32), jnp.asarray(pk, jnp.int32),
      qt, kx, vt, kxm, vmt, z, g)


def _outproj_kernel(x_ref, ya_ref, yc_ref, w_ref, g_ref, o_ref):
    hres = (x_ref[0]
            + jnp.dot(ya_ref[0], w_ref[0:D_ATTN, :], preferred_element_type=_F32)
            + jnp.dot(yc_ref[0], w_ref[D_ATTN:, :], preferred_element_type=_F32))
    ms = jnp.mean(hres * hres, axis=-1, keepdims=True)
    o_ref[0] = hres * lax.rsqrt(ms + EPS) * g_ref[...]


def _outproj(x, ya, yc, w_out, g, *, tm):
    b, l, d = x.shape
    row_blk = lambda bi, i: (bi, i, 0)
    const = lambda bi, i: (0, 0)
    return pl.pallas_call(
        _outproj_kernel,
        grid=(b, l // tm),
        in_specs=[
            pl.BlockSpec((1, tm, d), row_blk),
            pl.BlockSpec((1, tm, D_ATTN), row_blk),
            pl.BlockSpec((1, tm, D_CONV), row_blk),
            pl.BlockSpec(w_out.shape, const),
            pl.BlockSpec((1, d), const),
        ],
        out_specs=pl.BlockSpec((1, tm, d), row_blk),
        out_shape=jax.ShapeDtypeStruct((b, l, d), _F32),
        compiler_params=pltpu.CompilerParams(
            dimension_semantics=("arbitrary", "arbitrary"),
            vmem_limit_bytes=VMEM_LIMIT),
        name="outproj",
    )(x, ya, yc, w_out, g)


def kernel(x, meta, norm_g, w_in, b_f, conv_w, attn_norm_g, conv_norm_g, w_out, final_norm_g):
    assert norm_g.shape[0] == 1, "single layer only"
    b, seq, d = x.shape
    w = w_in[0]
    f0 = 3 * D_ATTN
    w_main = jnp.concatenate([w[:, D_ATTN:2 * D_ATTN], w[:, f0 + ATTN_HEADS:]],
                             axis=1).astype(_BF16)
    wqv_t = jnp.concatenate(
        [w[:, :D_ATTN], w[:, 2 * D_ATTN:f0 + ATTN_HEADS], jnp.zeros((d, ATTN_HEADS), w.dtype)],
        axis=1).T.astype(_BF16)
    bf = b_f[0][:, None]
    cw = conv_w[0]

    meta_rows = LANES
    meta_p = jnp.pad(meta.astype(x.dtype), ((0, meta_rows - N_META), (0, 0)))[None]
    zero_halo = jnp.zeros((SUBLANES, D_CONV), _F32)
    zero_carry = jnp.zeros((ATTN_HEADS, 1), _F32)
    _, kxm, vmt, _, _, cxm, crow = _inproj(
        meta_p, norm_g, w_main, wqv_t, bf, cw, conv_norm_g, zero_halo, zero_carry,
        tm=meta_rows, emit_tail=True)
    kxm = kxm[0, :N_META]
    vmt = vmt[0, :, :N_META]
    halo0 = cxm[0, N_META - SUBLANES:N_META]
    carry0 = crow[0, :, N_META - 1:N_META]

    qt, kx, vt, z, yc = _inproj(x, norm_g, w_main, wqv_t, bf, cw, conv_norm_g,
                                halo0, carry0, tm=512, emit_tail=False)
    ya = _attention(qt, kx, vt, kxm, vmt, z, attn_norm_g, tq=512)
    return _outproj(x, ya, yc, w_out[0].astype(_BF16), final_norm_g[None, :], tm=1024)
```

```python
import functools
import math

import jax
import jax.numpy as jnp
from jax import lax
from jax.experimental import pallas as pl
from jax.experimental.pallas import tpu as pltpu

D_MODEL = 1024
N_META = 16
ATTN_HEADS = 8
HEAD_DIM = 64
D_ATTN = 512
D_CONV = 512
CONV_WIDTH = 3
EPS = 1e-6
LANES = 128
SUBLANES = 8
N_PIECES = 3
ONES_ROWS = 16
TICKS_PER_TRIP = 4
LOG2E = math.log2(math.e)
VMEM_LIMIT = 48 * 1024 * 1024

_BF16 = jnp.bfloat16
_F32 = jnp.float32


def _log_sigmoid(x):
    return jnp.minimum(x, 0.0) - jnp.log1p(jnp.exp(-jnp.abs(x)))


def _silu(x):
    return x * (1.0 / (1.0 + jnp.exp(-x)))


def _dot_nt(a, b):
    return lax.dot_general(a, b, (((1,), (1,)), ((), ())),
                           preferred_element_type=_F32)


def _bf16_pieces(x):
    hi = x.astype(_BF16).astype(_F32)
    r1 = x - hi
    mid = r1.astype(_BF16).astype(_F32)
    return hi, mid, r1 - mid


def _inproj_kernel(x_ref, g_ref, w_ref, wqv_ref, bf_ref, cw_ref, cg_ref, halo0_ref,
                   carry0_ref, qt_ref, kx_ref, vt_ref, z_ref, yc_ref, *rest,
                   tm, emit_tail):
    if emit_tail:
        cx_ref, crow_ref, halo_s, carry_s, cx_s = rest
    else:
        halo_s, carry_s, cx_s = rest
    i = pl.program_id(1)

    @pl.when(i == 0)
    def _():
        halo_s[...] = halo0_ref[...]
        carry_s[...] = jnp.broadcast_to(carry0_ref[...], (ATTN_HEADS, LANES))

    x = x_ref[0]
    ms = jnp.mean(x * x, axis=-1, keepdims=True)
    u = (x * lax.rsqrt(ms + EPS) * g_ref[...]).astype(_BF16)

    def proj(c):
        return jnp.dot(u, w_ref[:, c * 512:(c + 1) * 512],
                       preferred_element_type=_F32)

    gate_b = proj(2)
    cx = proj(3) * proj(4)
    cx_s[0:SUBLANES, :] = halo_s[...]
    cx_s[SUBLANES:SUBLANES + tm, :] = cx
    if emit_tail:
        cx_ref[0] = cx
    conv = (cw_ref[0:1, :] * cx_s[SUBLANES - 2:SUBLANES - 2 + tm, :]
            + cw_ref[1:2, :] * cx_s[SUBLANES - 1:SUBLANES - 1 + tm, :]
            + cw_ref[2:3, :] * cx)
    halo_s[...] = cx[tm - SUBLANES:tm, :]
    yb = gate_b * conv
    zc = proj(5)
    low = lax.broadcasted_iota(jnp.int32, (1, LANES), 1) < HEAD_DIM
    for cb in range(D_CONV // LANES):
        sl = slice(cb * LANES, (cb + 1) * LANES)
        y = yb[:, sl]
        y2 = y * y
        s_lo = jnp.sum(jnp.where(low, y2, 0.0), axis=-1, keepdims=True)
        s_hi = jnp.sum(jnp.where(low, 0.0, y2), axis=-1, keepdims=True)
        msq = jnp.where(low, s_lo, s_hi) * (1.0 / HEAD_DIM)
        yn = y * lax.rsqrt(msq + EPS) * cg_ref[:, sl]
        yc_ref[0, :, sl] = (yn * _silu(zc[:, sl])).astype(_BF16)

    qvt = _dot_nt(wqv_ref[...], u)
    qt_ref[0] = (qvt[:D_ATTN] * (LOG2E * HEAD_DIM ** -0.5)).astype(_BF16)
    vt_ref[0] = qvt[D_ATTN:2 * D_ATTN].astype(_BF16)

    logf = _log_sigmoid(qvt[2 * D_ATTN:2 * D_ATTN + ATTN_HEADS] + bf_ref[...])
    hi, mid, lo = _bf16_pieces(logf)
    pieces = jnp.concatenate([hi, mid, lo, jnp.zeros_like(hi)], axis=0).astype(_BF16)
    row = lax.broadcasted_iota(jnp.int32, (tm, tm), 0)
    col = lax.broadcasted_iota(jnp.int32, (tm, tm), 1)
    tri = jnp.where(row <= col, 1.0, 0.0).astype(_BF16)
    c3 = jnp.dot(pieces, tri, preferred_element_type=_F32)
    h8 = ATTN_HEADS
    csum = c3[0:h8] + c3[h8:2 * h8] + c3[2 * h8:3 * h8] + carry_s[:, 0:1]
    carry_s[...] = jnp.broadcast_to(csum[:, tm - 1:tm], (h8, LANES))
    if emit_tail:
        crow_ref[0] = csum
    hi, mid, lo = _bf16_pieces(csum * (-LOG2E))
    aug = jnp.concatenate(
        [jnp.zeros((HEAD_DIM, tm), _F32), hi, mid, lo,
         jnp.zeros((LANES - HEAD_DIM - N_PIECES * h8, tm), _F32)], axis=0).T

    k = proj(0)
    for p in range(D_ATTN // LANES):
        kp = k[:, p * LANES:(p + 1) * LANES]
        sl = slice(2 * p * LANES, (2 * p + 1) * LANES)
        kx_ref[0, :, sl] = jnp.where(low, kp, aug).astype(_BF16)
        sl = slice((2 * p + 1) * LANES, (2 * p + 2) * LANES)
        kx_ref[0, :, sl] = jnp.where(low, pltpu.roll(kp, HEAD_DIM, axis=1), aug).astype(_BF16)
    z_ref[0] = proj(1).astype(_BF16)


def _inproj(x3, g, w_main, wqv_t, bf, cw, cg, halo0, carry0, *, tm, emit_tail):
    b, l, d = x3.shape
    nt = l // tm
    kern = functools.partial(_inproj_kernel, tm=tm, emit_tail=emit_tail)
    const = lambda bi, i: (0, 0)
    row_blk = lambda bi, i: (bi, i, 0)
    col_blk = lambda bi, i: (bi, 0, i)
    rows = (jax.ShapeDtypeStruct((b, l, 512), _BF16), pl.BlockSpec((1, tm, 512), row_blk))
    cols = (jax.ShapeDtypeStruct((b, 512, l), _BF16), pl.BlockSpec((1, 512, tm), col_blk))
    keys = (jax.ShapeDtypeStruct((b, l, ATTN_HEADS * LANES), _BF16),
            pl.BlockSpec((1, tm, ATTN_HEADS * LANES), row_blk))
    out_shape, out_specs = (list(t) for t in zip(cols, keys, cols, rows, rows))
    if emit_tail:
        out_shape.append(jax.ShapeDtypeStruct((b, l, D_CONV), _F32))
        out_specs.append(pl.BlockSpec((1, tm, D_CONV), row_blk))
        out_shape.append(jax.ShapeDtypeStruct((b, ATTN_HEADS, l), _F32))
        out_specs.append(pl.BlockSpec((1, ATTN_HEADS, tm), col_blk))
    return pl.pallas_call(
        kern,
        grid=(b, nt),
        in_specs=[
            pl.BlockSpec((1, tm, d), row_blk),
            pl.BlockSpec((1, d), const),
            pl.BlockSpec(w_main.shape, const),
            pl.BlockSpec(wqv_t.shape, const),
            pl.BlockSpec((ATTN_HEADS, 1), const),
            pl.BlockSpec((CONV_WIDTH, D_CONV), const),
            pl.BlockSpec((1, D_CONV), const),
            pl.BlockSpec((SUBLANES, D_CONV), const),
            pl.BlockSpec((ATTN_HEADS, 1), const),
        ],
        out_specs=out_specs,
        out_shape=out_shape,
        scratch_shapes=[
            pltpu.VMEM((SUBLANES, D_CONV), _F32),
            pltpu.VMEM((ATTN_HEADS, LANES), _F32),
            pltpu.VMEM((tm + SUBLANES, D_CONV), _F32),
        ],
        compiler_params=pltpu.CompilerParams(
            dimension_semantics=("arbitrary", "arbitrary"),
            vmem_limit_bytes=VMEM_LIMIT),
        name="inproj_meta" if emit_tail else "inproj",
    )(x3, g, w_main, wqv_t, bf, cw, cg, halo0, carry0)


def _attn_kernel(pq_ref, pk_ref, qt_ref, kx_ref, vt_ref, kxm_ref, vmt_ref,
                 z_ref, g_ref, o_ref, q2t_s, sd1_buf, sd2_buf, su_buf, mbd1_s, mbd2_s, mbu_s,
                 mask1_s, mask2_s, m_s, acc_s, *, tq, n_q, n_pairs):
    tk = tq
    th = tq // 2
    pair = pl.program_id(1)
    sub = lax.broadcasted_iota(jnp.int32, (LANES, 1), 0)
    top = sub < HEAD_DIM
    ones_v = jnp.ones((ONES_ROWS, tk), _BF16)
    ones_m = jnp.ones((ONES_ROWS, N_META), _BF16)

    @pl.when(jnp.logical_and(pl.program_id(0) == 0, pair == 0))
    def _():
        c = lax.broadcasted_iota(jnp.int32, (th, 2 * tq), 1)
        q_loc = jnp.where(c >= tq, c - tq, c)
        k_loc = lax.broadcasted_iota(jnp.int32, (th, 2 * tq), 0)
        mask1_s[0:th, :] = jnp.where(k_loc <= q_loc, 0.0, -jnp.inf)
        mask1_s[th:, :] = jnp.zeros((N_META, 2 * tq), _F32)
        c = lax.broadcasted_iota(jnp.int32, (th, 2 * th), 1)
        q_loc = jnp.where(c >= th, c - th, c)
        k_loc = lax.broadcasted_iota(jnp.int32, (th, 2 * th), 0)
        mask2_s[...] = jnp.where(k_loc <= q_loc, 0.0, -jnp.inf)

    def ones_for(h):
        r = lax.broadcasted_iota(jnp.int32, (HEAD_DIM, 1), 0)
        sel = jnp.logical_and(r % ATTN_HEADS == h, r < N_PIECES * ATTN_HEADS)
        return jnp.broadcast_to(jnp.where(sel, 1.0, 0.0).astype(_BF16), (HEAD_DIM, tq))

    def build_q(qi, carry):
        qt = qt_ref[0, :, pl.ds(pl.multiple_of(qi * tq, tq), tq)]
        q2t_s[qi, 0:HEAD_DIM, 0:tq] = qt[0:HEAD_DIM]
        q2t_s[qi, 0:HEAD_DIM, tq:] = qt[HEAD_DIM:]
        q2t_s[qi, HEAD_DIM:, 0:tq] = ones_for(2 * pair)
        q2t_s[qi, HEAD_DIM:, tq:] = ones_for(2 * pair + 1)
        return carry

    lax.fori_loop(0, n_q, build_q, 0)

    def keys(j):
        return kx_ref[0, pl.ds(pl.multiple_of(j * tk, tk), tk), :]

    def logits(kx, qa, qb):
        return jnp.concatenate(
            [jnp.dot(kx[:, :LANES], qa, preferred_element_type=_F32),
             jnp.dot(kx[:, LANES:], qb, preferred_element_type=_F32)], axis=1)

    def values(j):
        start = pl.multiple_of(j * tk, tk)
        return jnp.concatenate([vt_ref[0, :, pl.ds(start, tk)], ones_v], axis=0)

    def late(x):
        return jnp.concatenate([x[..., th:tq], x[..., tq + th:]], axis=-1)

    def produce_d1(qi, slot):
        kx = jnp.concatenate([kx_ref[0, qi * tk:qi * tk + th, :], kxm_ref[...]], axis=0)
        q2t = q2t_s[qi]
        s = logits(kx, q2t[:, :tq], q2t[:, tq:]) + mask1_s[...]
        sd1_buf[slot] = s
        mbd1_s[slot] = jnp.max(s, axis=0, keepdims=True)

    def consume_d1(qi, slot):
        m = mbd1_s[slot]
        p = jnp.exp2(sd1_buf[slot] - m).astype(_BF16)
        m_s[qi] = m
        vals = jnp.concatenate([vt_ref[0, :, qi * tk:qi * tk + th], ones_v[:, :th]], axis=0)
        acc_s[qi] = (jnp.dot(vals, p[0:th], preferred_element_type=_F32)
                     + jnp.dot(jnp.concatenate([vmt_ref[...], ones_m], axis=0), p[th:],
                               preferred_element_type=_F32))

    def produce_d2(qi, slot):
        kx = kx_ref[0, qi * tk + th:(qi + 1) * tk, :]
        q2t = q2t_s[qi]
        s = logits(kx, q2t[:, th:tq], q2t[:, tq + th:]) + mask2_s[...]
        sd2_buf[slot] = s
        mbd2_s[slot] = jnp.max(s, axis=0, keepdims=True)

    def consume_d2(qi, slot):
        m = late(m_s[qi])
        m_new = jnp.maximum(m, mbd2_s[slot])
        alpha = jnp.exp2(m - m_new)
        p = jnp.exp2(sd2_buf[slot] - m_new).astype(_BF16)
        vals = jnp.concatenate([vt_ref[0, :, qi * tk + th:(qi + 1) * tk], ones_v[:, :th]],
                               axis=0)
        acc = alpha * late(acc_s[qi]) + jnp.dot(vals, p, preferred_element_type=_F32)
        m_s[qi, :, th:tq] = m_new[:, :th]
        m_s[qi, :, tq + th:] = m_new[:, th:]
        acc_s[qi, :, th:tq] = acc[:, :th]
        acc_s[qi, :, tq + th:] = acc[:, th:]

    def produce(t, slot):
        q2t = q2t_s[pq_ref[t]]
        s = logits(keys(pk_ref[t]), q2t[:, :tq], q2t[:, tq:])
        su_buf[slot] = s
        mbu_s[slot] = jnp.max(s, axis=0, keepdims=True)

    def consume(t, slot):
        qi = pq_ref[t]
        m = m_s[qi]
        m_new = jnp.maximum(m, mbu_s[slot])
        alpha = jnp.exp2(m - m_new)
        p = jnp.exp2(su_buf[slot] - m_new).astype(_BF16)
        m_s[qi] = m_new
        acc_s[qi] = alpha * acc_s[qi] + jnp.dot(values(pk_ref[t]), p,
                                                preferred_element_type=_F32)

    def finalize(qi, carry):
        rows = pl.ds(pl.multiple_of(qi * tq, tq), tq)
        acc = acc_s[qi]
        ot = acc[0:LANES] * (1.0 / acc[LANES:LANES + 1])
        oc = jnp.where(top, ot[:, :tq], ot[:, tq:])
        o2 = oc * oc
        msa = jnp.sum(jnp.where(top, o2, 0.0), axis=0, keepdims=True)
        msb = jnp.sum(jnp.where(top, 0.0, o2), axis=0, keepdims=True)
        inv = jnp.where(top, lax.rsqrt(msa * (1.0 / HEAD_DIM) + EPS),
                        lax.rsqrt(msb * (1.0 / HEAD_DIM) + EPS))
        y = (oc * inv).T * g_ref[...]
        z = z_ref[0, rows, :].astype(_F32)
        o_ref[0, rows, :] = (y * _silu(z)).astype(_BF16)
        return carry

    produce_d1(0, 0)
    for qi in range(n_q):
        produce_d2(qi, qi % 2)
        consume_d1(qi, qi % 2)
        if qi + 1 < n_q:
            produce_d1(qi + 1, (qi + 1) % 2)
        else:
            produce(0, 0)
        consume_d2(qi, qi % 2)

    def ticks(t0, n):
        for d in range(n):
            produce(t0 + d, (1 + d) % 2)
            consume(t0 + d - 1, d % 2)

    def tick_group(u, carry):
        ticks(TICKS_PER_TRIP * u + 1, TICKS_PER_TRIP)
        return carry

    n_trips = (n_pairs - 1) // TICKS_PER_TRIP
    lax.fori_loop(0, n_trips, tick_group, 0)
    ticks(TICKS_PER_TRIP * n_trips + 1, (n_pairs - 1) % TICKS_PER_TRIP)
    consume(n_pairs - 1, (n_pairs - 1) % 2)

    lax.fori_loop(0, n_q, finalize, 0)


def _attention(qt, kx, vt, kxm, vmt, z, g, *, tq):
    b, l, _ = z.shape
    n_pair = D_ATTN // LANES
    n_q = l // tq
    assert n_q % 2 == 0 and n_q >= 2
    pq = [qi for qi in range(n_q) for _ in range(qi)]
    pk = [j for qi in range(n_q) for j in range(qi)]
    n_pairs = len(pq)
    assert n_pairs % 2 == 0 and n_pairs >= 2
    kern = functools.partial(_attn_kernel, tq=tq, n_q=n_q, n_pairs=n_pairs)
    seq_rows = pl.BlockSpec((1, l, LANES), lambda bi, p, *_: (bi, 0, p))
    seq_cols = pl.BlockSpec((1, LANES, l), lambda bi, p, *_: (bi, p, 0))
    th = tq // 2
    grid_spec = pltpu.PrefetchScalarGridSpec(
        num_scalar_prefetch=2,
        grid=(b, n_pair),
        in_specs=[
            seq_cols,
            pl.BlockSpec((1, l, 2 * LANES), lambda bi, p, *_: (bi, 0, p)),
            seq_cols,
            pl.BlockSpec((N_META, 2 * LANES), lambda bi, p, *_: (0, p)),
            pl.BlockSpec((LANES, N_META), lambda bi, p, *_: (p, 0)),
            seq_rows,
            pl.BlockSpec((1, LANES), lambda bi, p, *_: (0, p)),
        ],
        out_specs=seq_rows,
        scratch_shapes=[
            pltpu.VMEM((n_q, LANES, 2 * tq), _BF16),
            pltpu.VMEM((2, th + N_META, 2 * tq), _F32),
            pltpu.VMEM((2, th, 2 * th), _F32),
            pltpu.VMEM((2, tq, 2 * tq), _F32),
            pltpu.VMEM((2, 1, 2 * tq), _F32),
            pltpu.VMEM((2, 1, 2 * th), _F32),
            pltpu.VMEM((2, 1, 2 * tq), _F32),
            pltpu.VMEM((th + N_META, 2 * tq), _F32),
            pltpu.VMEM((th, 2 * th), _F32),
            pltpu.VMEM((n_q, 1, 2 * tq), _F32),
            pltpu.VMEM((n_q, LANES + ONES_ROWS, 2 * tq), _F32),
        ],
    )
    return pl.pallas_call(
        kern,
        grid_spec=grid_spec,
        out_shape=jax.ShapeDtypeStruct((b, l, D_ATTN), _BF16),
        compiler_params=pltpu.CompilerParams(
            dimension_semantics=("arbitrary", "arbitrary"),
            vmem_limit_bytes=VMEM_LIMIT),
        name="fox_attention",
    )(jnp.asarray(pq, jnp.int32), jnp.asarray(pk, jnp.int32),
      qt, kx, vt, kxm, vmt, z, g)


def _outproj_kernel(x_ref, ya_ref, yc_ref, w_ref, g_ref, o_ref):
    hres = (x_ref[0]
            + jnp.dot(ya_ref[0], w_ref[0:D_ATTN, :], preferred_element_type=_F32)
            + jnp.dot(yc_ref[0], w_ref[D_ATTN:, :], preferred_element_type=_F32))
    ms = jnp.mean(hres * hres, axis=-1, keepdims=True)
    o_ref[0] = hres * lax.rsqrt(ms + EPS) * g_ref[...]


def _outproj(x, ya, yc, w_out, g, *, tm):
    b, l, d = x.shape
    row_blk = lambda bi, i: (bi, i, 0)
    const = lambda bi, i: (0, 0)
    return pl.pallas_call(
        _outproj_kernel,
        grid=(b, l // tm),
        in_specs=[
            pl.BlockSpec((1, tm, d), row_blk),
            pl.BlockSpec((1, tm, D_ATTN), row_blk),
            pl.BlockSpec((1, tm, D_CONV), row_blk),
            pl.BlockSpec(w_out.shape, const),
            pl.BlockSpec((1, d), const),
        ],
        out_specs=pl.BlockSpec((1, tm, d), row_blk),
        out_shape=jax.ShapeDtypeStruct((b, l, d), _F32),
        compiler_params=pltpu.CompilerParams(
            dimension_semantics=("arbitrary", "arbitrary"),
            vmem_limit_bytes=VMEM_LIMIT),
        name="outproj",
    )(x, ya, yc, w_out, g)


def kernel(x, meta, norm_g, w_in, b_f, conv_w, attn_norm_g, conv_norm_g, w_out, final_norm_g):
    assert norm_g.shape[0] == 1, "single layer only"
    b, seq, d = x.shape
    w = w_in[0]
    f0 = 3 * D_ATTN
    w_main = jnp.concatenate([w[:, D_ATTN:2 * D_ATTN], w[:, f0 + ATTN_HEADS:]],
                             axis=1).astype(_BF16)
    wqv_t = jnp.concatenate(
        [w[:, :D_ATTN], w[:, 2 * D_ATTN:f0 + ATTN_HEADS], jnp.zeros((d, ATTN_HEADS), w.dtype)],
        axis=1).T.astype(_BF16)
    bf = b_f[0][:, None]
    cw = conv_w[0]

    meta_rows = LANES
    meta_p = jnp.pad(meta.astype(x.dtype), ((0, meta_rows - N_META), (0, 0)))[None]
    zero_halo = jnp.zeros((SUBLANES, D_CONV), _F32)
    zero_carry = jnp.zeros((ATTN_HEADS, 1), _F32)
    _, kxm, vmt, _, _, cxm, crow = _inproj(
        meta_p, norm_g, w_main, wqv_t, bf, cw, conv_norm_g, zero_halo, zero_carry,
        tm=meta_rows, emit_tail=True)
    kxm = kxm[0, :N_META]
    vmt = vmt[0, :, :N_META]
    halo0 = cxm[0, N_META - SUBLANES:N_META]
    carry0 = crow[0, :, N_META - 1:N_META]

    qt, kx, vt, z, yc = _inproj(x, norm_g, w_main, wqv_t, bf, cw, conv_norm_g,
                                halo0, carry0, tm=1024, emit_tail=False)
    ya = _attention(qt, kx, vt, kxm, vmt, z, attn_norm_g, tq=512)
    return _outproj(x, ya, yc, w_out[0].astype(_BF16), final_norm_g[None, :], tm=1024)
```

```python
import functools
import math

import jax
import jax.numpy as jnp
from jax import lax
from jax.experimental import pallas as pl
from jax.experimental.pallas import tpu as pltpu

D_MODEL = 1024
N_META = 16
ATTN_HEADS = 8
HEAD_DIM = 64
D_ATTN = 512
D_CONV = 512
CONV_WIDTH = 3
EPS = 1e-6
LANES = 128
SUBLANES = 8
N_PIECES = 3
ONES_ROWS = 16
TICKS_PER_TRIP = 8
LOG2E = math.log2(math.e)
VMEM_LIMIT = 48 * 1024 * 1024

_BF16 = jnp.bfloat16
_F32 = jnp.float32


def _log_sigmoid(x):
    return jnp.minimum(x, 0.0) - jnp.log1p(jnp.exp(-jnp.abs(x)))


def _silu(x):
    return x * (1.0 / (1.0 + jnp.exp(-x)))


def _dot_nt(a, b):
    return lax.dot_general(a, b, (((1,), (1,)), ((), ())),
                           preferred_element_type=_F32)


def _bf16_pieces(x):
    hi = x.astype(_BF16).astype(_F32)
    r1 = x - hi
    mid = r1.astype(_BF16).astype(_F32)
    return hi, mid, r1 - mid


def _wprep_kernel(w_ref, wm_ref, wt_ref):
    w = w_ref[0]
    f0 = 3 * D_ATTN
    wm_ref[:, :D_ATTN] = w[:, D_ATTN:2 * D_ATTN].astype(_BF16)
    wm_ref[:, D_ATTN:] = w[:, f0 + ATTN_HEADS:].astype(_BF16)
    wt_ref[0:D_ATTN, :] = w[:, :D_ATTN].T.astype(_BF16)
    wt_ref[D_ATTN:2 * D_ATTN, :] = w[:, 2 * D_ATTN:f0].T.astype(_BF16)
    ft = w[:, f0:f0 + LANES].T
    row = lax.broadcasted_iota(jnp.int32, (2 * ATTN_HEADS, 1), 0)
    wt_ref[2 * D_ATTN:, :] = jnp.where(row < ATTN_HEADS, ft[:2 * ATTN_HEADS], 0.0).astype(_BF16)


def _wprep(w_in, *, rows):
    _, d, d_in = w_in.shape
    n_main = d_in - 2 * D_ATTN - ATTN_HEADS
    n_t = 2 * D_ATTN + 2 * ATTN_HEADS
    return pl.pallas_call(
        _wprep_kernel,
        grid=(d // rows,),
        in_specs=[pl.BlockSpec((1, rows, d_in), lambda i: (0, i, 0))],
        out_specs=[pl.BlockSpec((rows, n_main), lambda i: (i, 0)),
                   pl.BlockSpec((n_t, rows), lambda i: (0, i))],
        out_shape=[jax.ShapeDtypeStruct((d, n_main), _BF16),
                   jax.ShapeDtypeStruct((n_t, d), _BF16)],
        compiler_params=pltpu.CompilerParams(
            dimension_semantics=("arbitrary",), vmem_limit_bytes=VMEM_LIMIT),
        name="wprep",
    )(w_in)


def _inproj_kernel(x_ref, g_ref, w_ref, wqv_ref, bf_ref, cw_ref, cg_ref, halo0_ref,
                   carry0_ref, qt_ref, kx_ref, vt_ref, z_ref, yc_ref, *rest,
                   tm, emit_tail):
    if emit_tail:
        cx_ref, crow_ref, halo_s, carry_s, cx_s = rest
    else:
        halo_s, carry_s, cx_s = rest
    i = pl.program_id(1)

    @pl.when(i == 0)
    def _():
        halo_s[...] = halo0_ref[...]
        carry_s[...] = jnp.broadcast_to(carry0_ref[...], (ATTN_HEADS, LANES))

    x = x_ref[0]
    ms = jnp.mean(x * x, axis=-1, keepdims=True)
    u = (x * lax.rsqrt(ms + EPS) * g_ref[...]).astype(_BF16)

    def proj(c):
        return jnp.dot(u, w_ref[:, c * 512:(c + 1) * 512],
                       preferred_element_type=_F32)

    gate_b = proj(2)
    cx = proj(3) * proj(4)
    cx_s[0:SUBLANES, :] = halo_s[...]
    cx_s[SUBLANES:SUBLANES + tm, :] = cx
    if emit_tail:
        cx_ref[0] = cx
    conv = (cw_ref[0:1, :] * cx_s[SUBLANES - 2:SUBLANES - 2 + tm, :]
            + cw_ref[1:2, :] * cx_s[SUBLANES - 1:SUBLANES - 1 + tm, :]
            + cw_ref[2:3, :] * cx)
    halo_s[...] = cx[tm - SUBLANES:tm, :]
    yb = gate_b * conv
    zc = proj(5)
    low = lax.broadcasted_iota(jnp.int32, (1, LANES), 1) < HEAD_DIM
    for cb in range(D_CONV // LANES):
        sl = slice(cb * LANES, (cb + 1) * LANES)
        y = yb[:, sl]
        y2 = y * y
        s_lo = jnp.sum(jnp.where(low, y2, 0.0), axis=-1, keepdims=True)
        s_hi = jnp.sum(jnp.where(low, 0.0, y2), axis=-1, keepdims=True)
        msq = jnp.where(low, s_lo, s_hi) * (1.0 / HEAD_DIM)
        yn = y * lax.rsqrt(msq + EPS) * cg_ref[:, sl]
        yc_ref[0, :, sl] = (yn * _silu(zc[:, sl])).astype(_BF16)

    qvt = _dot_nt(wqv_ref[...], u)
    qt_ref[0] = (qvt[:D_ATTN] * (LOG2E * HEAD_DIM ** -0.5)).astype(_BF16)
    vt_ref[0] = qvt[D_ATTN:2 * D_ATTN].astype(_BF16)

    logf = _log_sigmoid(qvt[2 * D_ATTN:2 * D_ATTN + ATTN_HEADS] + bf_ref[...])
    hi, mid, lo = _bf16_pieces(logf)
    pieces = jnp.concatenate([hi, mid, lo, jnp.zeros_like(hi)], axis=0).astype(_BF16)
    row = lax.broadcasted_iota(jnp.int32, (tm, tm), 0)
    col = lax.broadcasted_iota(jnp.int32, (tm, tm), 1)
    tri = jnp.where(row <= col, 1.0, 0.0).astype(_BF16)
    c3 = jnp.dot(pieces, tri, preferred_element_type=_F32)
    h8 = ATTN_HEADS
    csum = c3[0:h8] + c3[h8:2 * h8] + c3[2 * h8:3 * h8] + carry_s[:, 0:1]
    carry_s[...] = jnp.broadcast_to(csum[:, tm - 1:tm], (h8, LANES))
    if emit_tail:
        crow_ref[0] = csum
    hi, mid, lo = _bf16_pieces(csum * (-LOG2E))
    aug = jnp.concatenate(
        [jnp.zeros((HEAD_DIM, tm), _F32), hi, mid, lo,
         jnp.zeros((LANES - HEAD_DIM - N_PIECES * h8, tm), _F32)], axis=0).T

    k = proj(0)
    for p in range(D_ATTN // LANES):
        kp = k[:, p * LANES:(p + 1) * LANES]
        sl = slice(2 * p * LANES, (2 * p + 1) * LANES)
        kx_ref[0, :, sl] = jnp.where(low, kp, aug).astype(_BF16)
        sl = slice((2 * p + 1) * LANES, (2 * p + 2) * LANES)
        kx_ref[0, :, sl] = jnp.where(low, pltpu.roll(kp, HEAD_DIM, axis=1), aug).astype(_BF16)
    z_ref[0] = proj(1).astype(_BF16)


def _inproj(x3, g, w_main, wqv_t, bf, cw, cg, halo0, carry0, *, tm, emit_tail):
    b, l, d = x3.shape
    nt = l // tm
    kern = functools.partial(_inproj_kernel, tm=tm, emit_tail=emit_tail)
    const = lambda bi, i: (0, 0)
    row_blk = lambda bi, i: (bi, i, 0)
    col_blk = lambda bi, i: (bi, 0, i)
    rows = (jax.ShapeDtypeStruct((b, l, 512), _BF16), pl.BlockSpec((1, tm, 512), row_blk))
    cols = (jax.ShapeDtypeStruct((b, 512, l), _BF16), pl.BlockSpec((1, 512, tm), col_blk))
    keys = (jax.ShapeDtypeStruct((b, l, ATTN_HEADS * LANES), _BF16),
            pl.BlockSpec((1, tm, ATTN_HEADS * LANES), row_blk))
    out_shape, out_specs = (list(t) for t in zip(cols, keys, cols, rows, rows))
    if emit_tail:
        out_shape.append(jax.ShapeDtypeStruct((b, l, D_CONV), _F32))
        out_specs.append(pl.BlockSpec((1, tm, D_CONV), row_blk))
        out_shape.append(jax.ShapeDtypeStruct((b, ATTN_HEADS, l), _F32))
        out_specs.append(pl.BlockSpec((1, ATTN_HEADS, tm), col_blk))
    return pl.pallas_call(
        kern,
        grid=(b, nt),
        in_specs=[
            pl.BlockSpec((1, tm, d), row_blk),
            pl.BlockSpec((1, d), const),
            pl.BlockSpec(w_main.shape, const),
            pl.BlockSpec(wqv_t.shape, const),
            pl.BlockSpec((ATTN_HEADS, 1), const),
            pl.BlockSpec((CONV_WIDTH, D_CONV), const),
            pl.BlockSpec((1, D_CONV), const),
            pl.BlockSpec((SUBLANES, D_CONV), const),
            pl.BlockSpec((ATTN_HEADS, 1), const),
        ],
        out_specs=out_specs,
        out_shape=out_shape,
        scratch_shapes=[
            pltpu.VMEM((SUBLANES, D_CONV), _F32),
            pltpu.VMEM((ATTN_HEADS, LANES), _F32),
            pltpu.VMEM((tm + SUBLANES, D_CONV), _F32),
        ],
        compiler_params=pltpu.CompilerParams(
            dimension_semantics=("arbitrary", "arbitrary"),
            vmem_limit_bytes=VMEM_LIMIT),
        name="inproj_meta" if emit_tail else "inproj",
    )(x3, g, w_main, wqv_t, bf, cw, cg, halo0, carry0)


def _attn_kernel(pq_ref, pk_ref, qt_ref, kx_ref, vt_ref, kxm_ref, vmt_ref,
                 z_ref, g_ref, o_ref, q2t_s, sd1_buf, sd2_buf, su_buf, mbd1_s, mbd2_s, mbu_s,
                 mask1_s, mask2_s, m_s, acc_s, *, tq, n_q, n_pairs):
    tk = tq
    th = tq // 2
    pair = pl.program_id(1)
    sub = lax.broadcasted_iota(jnp.int32, (LANES, 1), 0)
    top = sub < HEAD_DIM
    ones_v = jnp.ones((ONES_ROWS, tk), _BF16)
    ones_m = jnp.ones((ONES_ROWS, N_META), _BF16)

    @pl.when(jnp.logical_and(pl.program_id(0) == 0, pair == 0))
    def _():
        c = lax.broadcasted_iota(jnp.int32, (th, 2 * tq), 1)
        q_loc = jnp.where(c >= tq, c - tq, c)
        k_loc = lax.broadcasted_iota(jnp.int32, (th, 2 * tq), 0)
        mask1_s[0:th, :] = jnp.where(k_loc <= q_loc, 0.0, -jnp.inf)
        mask1_s[th:, :] = jnp.zeros((N_META, 2 * tq), _F32)
        c = lax.broadcasted_iota(jnp.int32, (th, 2 * th), 1)
        q_loc = jnp.where(c >= th, c - th, c)
        k_loc = lax.broadcasted_iota(jnp.int32, (th, 2 * th), 0)
        mask2_s[...] = jnp.where(k_loc <= q_loc, 0.0, -jnp.inf)

    def ones_for(h):
        r = lax.broadcasted_iota(jnp.int32, (HEAD_DIM, 1), 0)
        sel = jnp.logical_and(r % ATTN_HEADS == h, r < N_PIECES * ATTN_HEADS)
        return jnp.broadcast_to(jnp.where(sel, 1.0, 0.0).astype(_BF16), (HEAD_DIM, tq))

    def build_q(qi, carry):
        qt = qt_ref[0, :, pl.ds(pl.multiple_of(qi * tq, tq), tq)]
        q2t_s[qi, 0:HEAD_DIM, 0:tq] = qt[0:HEAD_DIM]
        q2t_s[qi, 0:HEAD_DIM, tq:] = qt[HEAD_DIM:]
        q2t_s[qi, HEAD_DIM:, 0:tq] = ones_for(2 * pair)
        q2t_s[qi, HEAD_DIM:, tq:] = ones_for(2 * pair + 1)
        return carry

    lax.fori_loop(0, n_q, build_q, 0)

    def keys(j):
        return kx_ref[0, pl.ds(pl.multiple_of(j * tk, tk), tk), :]

    def logits(kx, qa, qb):
        return jnp.concatenate(
            [jnp.dot(kx[:, :LANES], qa, preferred_element_type=_F32),
             jnp.dot(kx[:, LANES:], qb, preferred_element_type=_F32)], axis=1)

    def values(j):
        start = pl.multiple_of(j * tk, tk)
        return jnp.concatenate([vt_ref[0, :, pl.ds(start, tk)], ones_v], axis=0)

    def late(x):
        return jnp.concatenate([x[..., th:tq], x[..., tq + th:]], axis=-1)

    def produce_d1(qi, slot):
        kx = jnp.concatenate([kx_ref[0, qi * tk:qi * tk + th, :], kxm_ref[...]], axis=0)
        q2t = q2t_s[qi]
        s = logits(kx, q2t[:, :tq], q2t[:, tq:]) + mask1_s[...]
        sd1_buf[slot] = s
        mbd1_s[slot] = jnp.max(s, axis=0, keepdims=True)

    def consume_d1(qi, slot):
        m = mbd1_s[slot]
        p = jnp.exp2(sd1_buf[slot] - m).astype(_BF16)
        m_s[qi] = m
        vals = jnp.concatenate([vt_ref[0, :, qi * tk:qi * tk + th], ones_v[:, :th]], axis=0)
        acc_s[qi] = (jnp.dot(vals, p[0:th], preferred_element_type=_F32)
                     + jnp.dot(jnp.concatenate([vmt_ref[...], ones_m], axis=0), p[th:],
                               preferred_element_type=_F32))

    def produce_d2(qi, slot):
        kx = kx_ref[0, qi * tk + th:(qi + 1) * tk, :]
        q2t = q2t_s[qi]
        s = logits(kx, q2t[:, th:tq], q2t[:, tq + th:]) + mask2_s[...]
        sd2_buf[slot] = s
        mbd2_s[slot] = jnp.max(s, axis=0, keepdims=True)

    def consume_d2(qi, slot):
        m = late(m_s[qi])
        m_new = jnp.maximum(m, mbd2_s[slot])
        alpha = jnp.exp2(m - m_new)
        p = jnp.exp2(sd2_buf[slot] - m_new).astype(_BF16)
        vals = jnp.concatenate([vt_ref[0, :, qi * tk + th:(qi + 1) * tk], ones_v[:, :th]],
                               axis=0)
        acc = alpha * late(acc_s[qi]) + jnp.dot(vals, p, preferred_element_type=_F32)
        m_s[qi, :, th:tq] = m_new[:, :th]
        m_s[qi, :, tq + th:] = m_new[:, th:]
        acc_s[qi, :, th:tq] = acc[:, :th]
        acc_s[qi, :, tq + th:] = acc[:, th:]

    def produce(t, slot):
        q2t = q2t_s[pq_ref[t]]
        s = logits(keys(pk_ref[t]), q2t[:, :tq], q2t[:, tq:])
        su_buf[slot] = s
        mbu_s[slot] = jnp.max(s, axis=0, keepdims=True)

    def consume(t, slot):
        qi = pq_ref[t]
        m = m_s[qi]
        m_new = jnp.maximum(m, mbu_s[slot])
        alpha = jnp.exp2(m - m_new)
        p = jnp.exp2(su_buf[slot] - m_new).astype(_BF16)
        m_s[qi] = m_new
        acc_s[qi] = alpha * acc_s[qi] + jnp.dot(values(pk_ref[t]), p,
                                                preferred_element_type=_F32)

    def finalize(qi, carry):
        rows = pl.ds(pl.multiple_of(qi * tq, tq), tq)
        acc = acc_s[qi]
        ot = acc[0:LANES] * (1.0 / acc[LANES:LANES + 1])
        oc = jnp.where(top, ot[:, :tq], ot[:, tq:])
        o2 = oc * oc
        msa = jnp.sum(jnp.where(top, o2, 0.0), axis=0, keepdims=True)
        msb = jnp.sum(jnp.where(top, 0.0, o2), axis=0, keepdims=True)
        inv = jnp.where(top, lax.rsqrt(msa * (1.0 / HEAD_DIM) + EPS),
                        lax.rsqrt(msb * (1.0 / HEAD_DIM) + EPS))
        y = (oc * inv).T * g_ref[...]
        z = z_ref[0, rows, :].astype(_F32)
        o_ref[0, rows, :] = (y * _silu(z)).astype(_BF16)
        return carry

    produce_d1(0, 0)
    for qi in range(n_q):
        produce_d2(qi, qi % 2)
        consume_d1(qi, qi % 2)
        if qi + 1 < n_q:
            produce_d1(qi + 1, (qi + 1) % 2)
        else:
            produce(0, 0)
        consume_d2(qi, qi % 2)

    def ticks(t0, n):
        for d in range(n):
            produce(t0 + d, (1 + d) % 2)
            consume(t0 + d - 1, d % 2)

    def tick_group(u, carry):
        ticks(TICKS_PER_TRIP * u + 1, TICKS_PER_TRIP)
        return carry

    n_trips = (n_pairs - 1) // TICKS_PER_TRIP
    lax.fori_loop(0, n_trips, tick_group, 0)
    ticks(TICKS_PER_TRIP * n_trips + 1, (n_pairs - 1) % TICKS_PER_TRIP)
    consume(n_pairs - 1, (n_pairs - 1) % 2)

    lax.fori_loop(0, n_q, finalize, 0)


def _attention(qt, kx, vt, kxm, vmt, z, g, *, tq):
    b, l, _ = z.shape
    n_pair = D_ATTN // LANES
    n_q = l // tq
    assert n_q % 2 == 0 and n_q >= 2
    pq = [qi for qi in range(n_q) for _ in range(qi)]
    pk = [j for qi in range(n_q) for j in range(qi)]
    n_pairs = len(pq)
    assert n_pairs % 2 == 0 and n_pairs >= 2
    kern = functools.partial(_attn_kernel, tq=tq, n_q=n_q, n_pairs=n_pairs)
    seq_rows = pl.BlockSpec((1, l, LANES), lambda bi, p, *_: (bi, 0, p))
    seq_cols = pl.BlockSpec((1, LANES, l), lambda bi, p, *_: (bi, p, 0))
    th = tq // 2
    grid_spec = pltpu.PrefetchScalarGridSpec(
        num_scalar_prefetch=2,
        grid=(b, n_pair),
        in_specs=[
            seq_cols,
            pl.BlockSpec((1, l, 2 * LANES), lambda bi, p, *_: (bi, 0, p)),
            seq_cols,
            pl.BlockSpec((N_META, 2 * LANES), lambda bi, p, *_: (0, p)),
            pl.BlockSpec((LANES, N_META), lambda bi, p, *_: (p, 0)),
            seq_rows,
            pl.BlockSpec((1, LANES), lambda bi, p, *_: (0, p)),
        ],
        out_specs=seq_rows,
        scratch_shapes=[
            pltpu.VMEM((n_q, LANES, 2 * tq), _BF16),
            pltpu.VMEM((2, th + N_META, 2 * tq), _F32),
            pltpu.VMEM((2, th, 2 * th), _F32),
            pltpu.VMEM((2, tq, 2 * tq), _F32),
            pltpu.VMEM((2, 1, 2 * tq), _F32),
            pltpu.VMEM((2, 1, 2 * th), _F32),
            pltpu.VMEM((2, 1, 2 * tq), _F32),
            pltpu.VMEM((th + N_META, 2 * tq), _F32),
            pltpu.VMEM((th, 2 * th), _F32),
            pltpu.VMEM((n_q, 1, 2 * tq), _F32),
            pltpu.VMEM((n_q, LANES + ONES_ROWS, 2 * tq), _F32),
        ],
    )
    return pl.pallas_call(
        kern,
        grid_spec=grid_spec,
        out_shape=jax.ShapeDtypeStruct((b, l, D_ATTN), _BF16),
        compiler_params=pltpu.CompilerParams(
            dimension_semantics=("arbitrary", "arbitrary"),
            vmem_limit_bytes=VMEM_LIMIT),
        name="fox_attention",
    )(jnp.asarray(pq, jnp.int32), jnp.asarray(pk, jnp.int32),
      qt, kx, vt, kxm, vmt, z, g)


def _outproj_kernel(x_ref, ya_ref, yc_ref, w_ref, g_ref, o_ref):
    hres = (x_ref[0]
            + jnp.dot(ya_ref[0], w_ref[0:D_ATTN, :], preferred_element_type=_F32)
            + jnp.dot(yc_ref[0], w_ref[D_ATTN:, :], preferred_element_type=_F32))
    ms = jnp.mean(hres * hres, axis=-1, keepdims=True)
    o_ref[0] = hres * lax.rsqrt(ms + EPS) * g_ref[...]


def _outproj(x, ya, yc, w_out, g, *, tm):
    b, l, d = x.shape
    row_blk = lambda bi, i: (bi, i, 0)
    const = lambda bi, i: (0, 0)
    return pl.pallas_call(
        _outproj_kernel,
        grid=(b, l // tm),
        in_specs=[
            pl.BlockSpec((1, tm, d), row_blk),
            pl.BlockSpec((1, tm, D_ATTN), row_blk),
            pl.BlockSpec((1, tm, D_CONV), row_blk),
            pl.BlockSpec(w_out.shape, const),
            pl.BlockSpec((1, d), const),
        ],
        out_specs=pl.BlockSpec((1, tm, d), row_blk),
        out_shape=jax.ShapeDtypeStruct((b, l, d), _F32),
        compiler_params=pltpu.CompilerParams(
            dimension_semantics=("arbitrary", "arbitrary"),
            vmem_limit_bytes=VMEM_LIMIT),
        name="outproj",
    )(x, ya, yc, w_out, g)


def kernel(x, meta, norm_g, w_in, b_f, conv_w, attn_norm_g, conv_norm_g, w_out, final_norm_g):
    assert norm_g.shape[0] == 1, "single layer only"
    b, seq, d = x.shape
    w_main, wqv_t = _wprep(w_in, rows=LANES)
    bf = b_f[0][:, None]
    cw = conv_w[0]

    meta_rows = LANES
    meta_p = jnp.pad(meta.astype(x.dtype), ((0, meta_rows - N_META), (0, 0)))[None]
    zero_halo = jnp.zeros((SUBLANES, D_CONV), _F32)
    zero_carry = jnp.zeros((ATTN_HEADS, 1), _F32)
    _, kxm, vmt, _, _, cxm, crow = _inproj(
        meta_p, norm_g, w_main, wqv_t, bf, cw, conv_norm_g, zero_halo, zero_carry,
        tm=meta_rows, emit_tail=True)
    kxm = kxm[0, :N_META]
    vmt = vmt[0, :, :N_META]
    halo0 = cxm[0, N_META - SUBLANES:N_META]
    carry0 = crow[0, :, N_META - 1:N_META]

    qt, kx, vt, z, yc = _inproj(x, norm_g, w_main, wqv_t, bf, cw, conv_norm_g,
                                halo0, carry0, tm=1024, emit_tail=False)
    ya = _attention(qt, kx, vt, kxm, vmt, z, attn_norm_g, tq=512)
    return _outproj(x, ya, yc, w_out[0].astype(_BF16), final_norm_g[None, :], tm=1024)
```

```python
import functools
import math

import jax
import jax.numpy as jnp
from jax import lax
from jax.experimental import pallas as pl
from jax.experimental.pallas import tpu as pltpu

D_MODEL = 1024
N_META = 16
ATTN_HEADS = 8
HEAD_DIM = 64
D_ATTN = 512
D_CONV = 512
CONV_WIDTH = 3
EPS = 1e-6
LANES = 128
SUBLANES = 8
N_PIECES = 3
QVF_ROWS = 2 * D_ATTN + 16
ONES_ROWS = 16
TICKS_PER_TRIP = 8
LOG2E = math.log2(math.e)
VMEM_LIMIT = 48 * 1024 * 1024

_BF16 = jnp.bfloat16
_F32 = jnp.float32


def _log_sigmoid(x):
    return jnp.minimum(x, 0.0) - jnp.log1p(jnp.exp(-jnp.abs(x)))


def _silu(x):
    return x * (1.0 / (1.0 + jnp.exp(-x)))


def _dot_nt(a, b):
    return lax.dot_general(a, b, (((1,), (1,)), ((), ())),
                           preferred_element_type=_F32)


def _bf16_pieces(x):
    hi = x.astype(_BF16).astype(_F32)
    r1 = x - hi
    mid = r1.astype(_BF16).astype(_F32)
    return hi, mid, r1 - mid


def _inproj_kernel(x_ref, g_ref, w_ref, bf_ref, cw_ref, cg_ref, halo0_ref,
                   carry0_ref, qt_ref, kx_ref, vt_ref, z_ref, yc_ref, *rest,
                   tm, emit_tail):
    if emit_tail:
        cx_ref, crow_ref, halo_s, carry_s, cx_s = rest
    else:
        halo_s, carry_s, cx_s = rest
    i = pl.program_id(1)

    @pl.when(i == 0)
    def _():
        halo_s[...] = halo0_ref[...]
        carry_s[...] = jnp.broadcast_to(carry0_ref[...], (ATTN_HEADS, LANES))

    x = x_ref[0]
    ms = jnp.mean(x * x, axis=-1, keepdims=True)
    u = (x * lax.rsqrt(ms + EPS) * g_ref[...]).astype(_BF16)

    def proj(c):
        lo = QVF_ROWS + c * 512
        return _dot_nt(u, w_ref[lo:lo + 512, :])

    gate_b = proj(2)
    cx = proj(3) * proj(4)
    cx_s[0:SUBLANES, :] = halo_s[...]
    cx_s[SUBLANES:SUBLANES + tm, :] = cx
    if emit_tail:
        cx_ref[0] = cx
    conv = (cw_ref[0:1, :] * cx_s[SUBLANES - 2:SUBLANES - 2 + tm, :]
            + cw_ref[1:2, :] * cx_s[SUBLANES - 1:SUBLANES - 1 + tm, :]
            + cw_ref[2:3, :] * cx)
    halo_s[...] = cx[tm - SUBLANES:tm, :]
    yb = gate_b * conv
    zc = proj(5)
    low = lax.broadcasted_iota(jnp.int32, (1, LANES), 1) < HEAD_DIM
    for cb in range(D_CONV // LANES):
        sl = slice(cb * LANES, (cb + 1) * LANES)
        y = yb[:, sl]
        y2 = y * y
        s_lo = jnp.sum(jnp.where(low, y2, 0.0), axis=-1, keepdims=True)
        s_hi = jnp.sum(jnp.where(low, 0.0, y2), axis=-1, keepdims=True)
        msq = jnp.where(low, s_lo, s_hi) * (1.0 / HEAD_DIM)
        yn = y * lax.rsqrt(msq + EPS) * cg_ref[:, sl]
        yc_ref[0, :, sl] = (yn * _silu(zc[:, sl])).astype(_BF16)

    qvt = _dot_nt(w_ref[0:QVF_ROWS, :], u)
    qt_ref[0] = (qvt[:D_ATTN] * (LOG2E * HEAD_DIM ** -0.5)).astype(_BF16)
    vt_ref[0] = qvt[D_ATTN:2 * D_ATTN].astype(_BF16)

    logf = _log_sigmoid(qvt[2 * D_ATTN:2 * D_ATTN + ATTN_HEADS] + bf_ref[...])
    hi, mid, lo = _bf16_pieces(logf)
    pieces = jnp.concatenate([hi, mid, lo, jnp.zeros_like(hi)], axis=0).astype(_BF16)
    row = lax.broadcasted_iota(jnp.int32, (tm, tm), 0)
    col = lax.broadcasted_iota(jnp.int32, (tm, tm), 1)
    tri = jnp.where(row <= col, 1.0, 0.0).astype(_BF16)
    c3 = jnp.dot(pieces, tri, preferred_element_type=_F32)
    h8 = ATTN_HEADS
    csum = c3[0:h8] + c3[h8:2 * h8] + c3[2 * h8:3 * h8] + carry_s[:, 0:1]
    carry_s[...] = jnp.broadcast_to(csum[:, tm - 1:tm], (h8, LANES))
    if emit_tail:
        crow_ref[0] = csum
    hi, mid, lo = _bf16_pieces(csum * (-LOG2E))
    aug = jnp.concatenate(
        [jnp.zeros((HEAD_DIM, tm), _F32), hi, mid, lo,
         jnp.zeros((LANES - HEAD_DIM - N_PIECES * h8, tm), _F32)], axis=0).T

    k = proj(0)
    for p in range(D_ATTN // LANES):
        kp = k[:, p * LANES:(p + 1) * LANES]
        sl = slice(2 * p * LANES, (2 * p + 1) * LANES)
        kx_ref[0, :, sl] = jnp.where(low, kp, aug).astype(_BF16)
        sl = slice((2 * p + 1) * LANES, (2 * p + 2) * LANES)
        kx_ref[0, :, sl] = jnp.where(low, pltpu.roll(kp, HEAD_DIM, axis=1), aug).astype(_BF16)
    z_ref[0] = proj(1).astype(_BF16)


def _inproj(x3, g, w_t, bf, cw, cg, halo0, carry0, *, tm, emit_tail):
    b, l, d = x3.shape
    nt = l // tm
    kern = functools.partial(_inproj_kernel, tm=tm, emit_tail=emit_tail)
    const = lambda bi, i: (0, 0)
    row_blk = lambda bi, i: (bi, i, 0)
    col_blk = lambda bi, i: (bi, 0, i)
    rows = (jax.ShapeDtypeStruct((b, l, 512), _BF16), pl.BlockSpec((1, tm, 512), row_blk))
    cols = (jax.ShapeDtypeStruct((b, 512, l), _BF16), pl.BlockSpec((1, 512, tm), col_blk))
    keys = (jax.ShapeDtypeStruct((b, l, ATTN_HEADS * LANES), _BF16),
            pl.BlockSpec((1, tm, ATTN_HEADS * LANES), row_blk))
    out_shape, out_specs = (list(t) for t in zip(cols, keys, cols, rows, rows))
    if emit_tail:
        out_shape.append(jax.ShapeDtypeStruct((b, l, D_CONV), _F32))
        out_specs.append(pl.BlockSpec((1, tm, D_CONV), row_blk))
        out_shape.append(jax.ShapeDtypeStruct((b, ATTN_HEADS, l), _F32))
        out_specs.append(pl.BlockSpec((1, ATTN_HEADS, tm), col_blk))
    return pl.pallas_call(
        kern,
        grid=(b, nt),
        in_specs=[
            pl.BlockSpec((1, tm, d), row_blk),
            pl.BlockSpec((1, d), const),
            pl.BlockSpec(w_t.shape, const),
            pl.BlockSpec((ATTN_HEADS, 1), const),
            pl.BlockSpec((CONV_WIDTH, D_CONV), const),
            pl.BlockSpec((1, D_CONV), const),
            pl.BlockSpec((SUBLANES, D_CONV), const),
            pl.BlockSpec((ATTN_HEADS, 1), const),
        ],
        out_specs=out_specs,
        out_shape=out_shape,
        scratch_shapes=[
            pltpu.VMEM((SUBLANES, D_CONV), _F32),
            pltpu.VMEM((ATTN_HEADS, LANES), _F32),
            pltpu.VMEM((tm + SUBLANES, D_CONV), _F32),
        ],
        compiler_params=pltpu.CompilerParams(
            dimension_semantics=("arbitrary", "arbitrary"),
            vmem_limit_bytes=VMEM_LIMIT),
        name="inproj_meta" if emit_tail else "inproj",
    )(x3, g, w_t, bf, cw, cg, halo0, carry0)


def _attn_kernel(pq_ref, pk_ref, qt_ref, kx_ref, vt_ref, kxm_ref, vmt_ref,
                 z_ref, g_ref, o_ref, q2t_s, sd1_buf, sd2_buf, su_buf, mbd1_s, mbd2_s, mbu_s,
                 mask1_s, mask2_s, m_s, acc_s, *, tq, n_q, n_pairs):
    tk = tq
    th = tq // 2
    pair = pl.program_id(1)
    sub = lax.broadcasted_iota(jnp.int32, (LANES, 1), 0)
    top = sub < HEAD_DIM
    ones_v = jnp.ones((ONES_ROWS, tk), _BF16)
    ones_m = jnp.ones((ONES_ROWS, N_META), _BF16)

    @pl.when(jnp.logical_and(pl.program_id(0) == 0, pair == 0))
    def _():
        c = lax.broadcasted_iota(jnp.int32, (th, 2 * tq), 1)
        q_loc = jnp.where(c >= tq, c - tq, c)
        k_loc = lax.broadcasted_iota(jnp.int32, (th, 2 * tq), 0)
        mask1_s[0:th, :] = jnp.where(k_loc <= q_loc, 0.0, -jnp.inf)
        mask1_s[th:, :] = jnp.zeros((N_META, 2 * tq), _F32)
        c = lax.broadcasted_iota(jnp.int32, (th, 2 * th), 1)
        q_loc = jnp.where(c >= th, c - th, c)
        k_loc = lax.broadcasted_iota(jnp.int32, (th, 2 * th), 0)
        mask2_s[...] = jnp.where(k_loc <= q_loc, 0.0, -jnp.inf)

    def ones_for(h):
        r = lax.broadcasted_iota(jnp.int32, (HEAD_DIM, 1), 0)
        sel = jnp.logical_and(r % ATTN_HEADS == h, r < N_PIECES * ATTN_HEADS)
        return jnp.broadcast_to(jnp.where(sel, 1.0, 0.0).astype(_BF16), (HEAD_DIM, tq))

    def build_q(qi, carry):
        qt = qt_ref[0, :, pl.ds(pl.multiple_of(qi * tq, tq), tq)]
        q2t_s[qi, 0:HEAD_DIM, 0:tq] = qt[0:HEAD_DIM]
        q2t_s[qi, 0:HEAD_DIM, tq:] = qt[HEAD_DIM:]
        q2t_s[qi, HEAD_DIM:, 0:tq] = ones_for(2 * pair)
        q2t_s[qi, HEAD_DIM:, tq:] = ones_for(2 * pair + 1)
        return carry

    lax.fori_loop(0, n_q, build_q, 0)

    def keys(j):
        return kx_ref[0, pl.ds(pl.multiple_of(j * tk, tk), tk), :]

    def logits(kx, qa, qb):
        return jnp.concatenate(
            [jnp.dot(kx[:, :LANES], qa, preferred_element_type=_F32),
             jnp.dot(kx[:, LANES:], qb, preferred_element_type=_F32)], axis=1)

    def values(j):
        start = pl.multiple_of(j * tk, tk)
        return jnp.concatenate([vt_ref[0, :, pl.ds(start, tk)], ones_v], axis=0)

    def late(x):
        return jnp.concatenate([x[..., th:tq], x[..., tq + th:]], axis=-1)

    def produce_d1(qi, slot):
        kx = jnp.concatenate([kx_ref[0, qi * tk:qi * tk + th, :], kxm_ref[...]], axis=0)
        q2t = q2t_s[qi]
        s = logits(kx, q2t[:, :tq], q2t[:, tq:]) + mask1_s[...]
        sd1_buf[slot] = s
        mbd1_s[slot] = jnp.max(s, axis=0, keepdims=True)

    def consume_d1(qi, slot):
        m = mbd1_s[slot]
        p = jnp.exp2(sd1_buf[slot] - m).astype(_BF16)
        m_s[qi] = m
        vals = jnp.concatenate([vt_ref[0, :, qi * tk:qi * tk + th], ones_v[:, :th]], axis=0)
        acc_s[qi] = (jnp.dot(vals, p[0:th], preferred_element_type=_F32)
                     + jnp.dot(jnp.concatenate([vmt_ref[...], ones_m], axis=0), p[th:],
                               preferred_element_type=_F32))

    def produce_d2(qi, slot):
        kx = kx_ref[0, qi * tk + th:(qi + 1) * tk, :]
        q2t = q2t_s[qi]
        s = logits(kx, q2t[:, th:tq], q2t[:, tq + th:]) + mask2_s[...]
        sd2_buf[slot] = s
        mbd2_s[slot] = jnp.max(s, axis=0, keepdims=True)

    def consume_d2(qi, slot):
        m = late(m_s[qi])
        m_new = jnp.maximum(m, mbd2_s[slot])
        alpha = jnp.exp2(m - m_new)
        p = jnp.exp2(sd2_buf[slot] - m_new).astype(_BF16)
        vals = jnp.concatenate([vt_ref[0, :, qi * tk + th:(qi + 1) * tk], ones_v[:, :th]],
                               axis=0)
        acc = alpha * late(acc_s[qi]) + jnp.dot(vals, p, preferred_element_type=_F32)
        m_s[qi, :, th:tq] = m_new[:, :th]
        m_s[qi, :, tq + th:] = m_new[:, th:]
        acc_s[qi, :, th:tq] = acc[:, :th]
        acc_s[qi, :, tq + th:] = acc[:, th:]

    def produce(t, slot):
        q2t = q2t_s[pq_ref[t]]
        s = logits(keys(pk_ref[t]), q2t[:, :tq], q2t[:, tq:])
        su_buf[slot] = s
        mbu_s[slot] = jnp.max(s, axis=0, keepdims=True)

    def consume(t, slot):
        qi = pq_ref[t]
        m = m_s[qi]
        m_new = jnp.maximum(m, mbu_s[slot])
        alpha = jnp.exp2(m - m_new)
        p = jnp.exp2(su_buf[slot] - m_new).astype(_BF16)
        m_s[qi] = m_new
        acc_s[qi] = alpha * acc_s[qi] + jnp.dot(values(pk_ref[t]), p,
                                                preferred_element_type=_F32)

    def finalize(qi, carry):
        rows = pl.ds(pl.multiple_of(qi * tq, tq), tq)
        acc = acc_s[qi]
        ot = acc[0:LANES] * (1.0 / acc[LANES:LANES + 1])
        oc = jnp.where(top, ot[:, :tq], ot[:, tq:])
        o2 = oc * oc
        msa = jnp.sum(jnp.where(top, o2, 0.0), axis=0, keepdims=True)
        msb = jnp.sum(jnp.where(top, 0.0, o2), axis=0, keepdims=True)
        inv = jnp.where(top, lax.rsqrt(msa * (1.0 / HEAD_DIM) + EPS),
                        lax.rsqrt(msb * (1.0 / HEAD_DIM) + EPS))
        y = (oc * inv).T * g_ref[...]
        z = z_ref[0, rows, :].astype(_F32)
        o_ref[0, rows, :] = (y * _silu(z)).astype(_BF16)
        return carry

    produce_d1(0, 0)
    for qi in range(n_q):
        produce_d2(qi, qi % 2)
        consume_d1(qi, qi % 2)
        if qi + 1 < n_q:
            produce_d1(qi + 1, (qi + 1) % 2)
        else:
            produce(0, 0)
        consume_d2(qi, qi % 2)

    def ticks(t0, n):
        for d in range(n):
            produce(t0 + d, (1 + d) % 2)
            consume(t0 + d - 1, d % 2)

    def tick_group(u, carry):
        ticks(TICKS_PER_TRIP * u + 1, TICKS_PER_TRIP)
        return carry

    n_trips = (n_pairs - 1) // TICKS_PER_TRIP
    lax.fori_loop(0, n_trips, tick_group, 0)
    ticks(TICKS_PER_TRIP * n_trips + 1, (n_pairs - 1) % TICKS_PER_TRIP)
    consume(n_pairs - 1, (n_pairs - 1) % 2)

    lax.fori_loop(0, n_q, finalize, 0)


def _attention(qt, kx, vt, kxm, vmt, z, g, *, tq):
    b, l, _ = z.shape
    n_pair = D_ATTN // LANES
    n_q = l // tq
    assert n_q % 2 == 0 and n_q >= 2
    pq = [qi for qi in range(n_q) for _ in range(qi)]
    pk = [j for qi in range(n_q) for j in range(qi)]
    n_pairs = len(pq)
    assert n_pairs % 2 == 0 and n_pairs >= 2
    kern = functools.partial(_attn_kernel, tq=tq, n_q=n_q, n_pairs=n_pairs)
    seq_rows = pl.BlockSpec((1, l, LANES), lambda bi, p, *_: (bi, 0, p))
    seq_cols = pl.BlockSpec((1, LANES, l), lambda bi, p, *_: (bi, p, 0))
    th = tq // 2
    grid_spec = pltpu.PrefetchScalarGridSpec(
        num_scalar_prefetch=2,
        grid=(b, n_pair),
        in_specs=[
            seq_cols,
            pl.BlockSpec((1, l, 2 * LANES), lambda bi, p, *_: (bi, 0, p)),
            seq_cols,
            pl.BlockSpec((N_META, 2 * LANES), lambda bi, p, *_: (0, p)),
            pl.BlockSpec((LANES, N_META), lambda bi, p, *_: (p, 0)),
            seq_rows,
            pl.BlockSpec((1, LANES), lambda bi, p, *_: (0, p)),
        ],
        out_specs=seq_rows,
        scratch_shapes=[
            pltpu.VMEM((n_q, LANES, 2 * tq), _BF16),
            pltpu.VMEM((2, th + N_META, 2 * tq), _F32),
            pltpu.VMEM((2, th, 2 * th), _F32),
            pltpu.VMEM((2, tq, 2 * tq), _F32),
            pltpu.VMEM((2, 1, 2 * tq), _F32),
            pltpu.VMEM((2, 1, 2 * th), _F32),
            pltpu.VMEM((2, 1, 2 * tq), _F32),
            pltpu.VMEM((th + N_META, 2 * tq), _F32),
            pltpu.VMEM((th, 2 * th), _F32),
            pltpu.VMEM((n_q, 1, 2 * tq), _F32),
            pltpu.VMEM((n_q, LANES + ONES_ROWS, 2 * tq), _F32),
        ],
    )
    return pl.pallas_call(
        kern,
        grid_spec=grid_spec,
        out_shape=jax.ShapeDtypeStruct((b, l, D_ATTN), _BF16),
        compiler_params=pltpu.CompilerParams(
            dimension_semantics=("arbitrary", "arbitrary"),
            vmem_limit_bytes=VMEM_LIMIT),
        name="fox_attention",
    )(jnp.asarray(pq, jnp.int32), jnp.asarray(pk, jnp.int32),
      qt, kx, vt, kxm, vmt, z, g)


def _outproj_kernel(x_ref, ya_ref, yc_ref, w_ref, g_ref, o_ref):
    hres = (x_ref[0]
            + jnp.dot(ya_ref[0], w_ref[0:D_ATTN, :], preferred_element_type=_F32)
            + jnp.dot(yc_ref[0], w_ref[D_ATTN:, :], preferred_element_type=_F32))
    ms = jnp.mean(hres * hres, axis=-1, keepdims=True)
    o_ref[0] = hres * lax.rsqrt(ms + EPS) * g_ref[...]


def _outproj(x, ya, yc, w_out, g, *, tm):
    b, l, d = x.shape
    row_blk = lambda bi, i: (bi, i, 0)
    const = lambda bi, i: (0, 0)
    return pl.pallas_call(
        _outproj_kernel,
        grid=(b, l // tm),
        in_specs=[
            pl.BlockSpec((1, tm, d), row_blk),
            pl.BlockSpec((1, tm, D_ATTN), row_blk),
            pl.BlockSpec((1, tm, D_CONV), row_blk),
            pl.BlockSpec(w_out.shape, const),
            pl.BlockSpec((1, d), const),
        ],
        out_specs=pl.BlockSpec((1, tm, d), row_blk),
        out_shape=jax.ShapeDtypeStruct((b, l, d), _F32),
        compiler_params=pltpu.CompilerParams(
            dimension_semantics=("arbitrary", "arbitrary"),
            vmem_limit_bytes=VMEM_LIMIT),
        name="outproj",
    )(x, ya, yc, w_out, g)


def kernel(x, meta, norm_g, w_in, b_f, conv_w, attn_norm_g, conv_norm_g, w_out, final_norm_g):
    assert norm_g.shape[0] == 1, "single layer only"
    b, seq, d = x.shape
    wt = w_in[0].T
    f0 = 3 * D_ATTN
    w_t = jnp.concatenate(
        [wt[:D_ATTN], wt[2 * D_ATTN:f0 + ATTN_HEADS], jnp.zeros((ATTN_HEADS, d), wt.dtype),
         wt[D_ATTN:2 * D_ATTN], wt[f0 + ATTN_HEADS:]], axis=0).astype(_BF16)
    bf = b_f[0][:, None]
    cw = conv_w[0]

    meta_rows = LANES
    meta_p = jnp.pad(meta.astype(x.dtype), ((0, meta_rows - N_META), (0, 0)))[None]
    zero_halo = jnp.zeros((SUBLANES, D_CONV), _F32)
    zero_carry = jnp.zeros((ATTN_HEADS, 1), _F32)
    _, kxm, vmt, _, _, cxm, crow = _inproj(
        meta_p, norm_g, w_t, bf, cw, conv_norm_g, zero_halo, zero_carry,
        tm=meta_rows, emit_tail=True)
    kxm = kxm[0, :N_META]
    vmt = vmt[0, :, :N_META]
    halo0 = cxm[0, N_META - SUBLANES:N_META]
    carry0 = crow[0, :, N_META - 1:N_META]

    qt, kx, vt, z, yc = _inproj(x, norm_g, w_t, bf, cw, conv_norm_g,
                                halo0, carry0, tm=1024, emit_tail=False)
    ya = _attention(qt, kx, vt, kxm, vmt, z, attn_norm_g, tq=512)
    return _outproj(x, ya, yc, w_out[0].astype(_BF16), final_norm_g[None, :], tm=1024)
```

```python
import functools
import math

import jax
import jax.numpy as jnp
from jax import lax
from jax.experimental import pallas as pl
from jax.experimental.pallas import tpu as pltpu

D_MODEL = 1024
N_META = 16
ATTN_HEADS = 8
HEAD_DIM = 64
D_ATTN = 512
D_CONV = 512
CONV_WIDTH = 3
EPS = 1e-6
LANES = 128
SUBLANES = 8
N_PIECES = 3
QVF_ROWS = 2 * D_ATTN + 16
ONES_ROWS = 16
TICKS_PER_TRIP = 8
LOG2E = math.log2(math.e)
VMEM_LIMIT = 48 * 1024 * 1024

_BF16 = jnp.bfloat16
_F32 = jnp.float32


def _log_sigmoid(x):
    return jnp.minimum(x, 0.0) - jnp.log1p(jnp.exp(-jnp.abs(x)))


def _silu(x):
    return x * (1.0 / (1.0 + jnp.exp(-x)))


def _dot_nt(a, b):
    return lax.dot_general(a, b, (((1,), (1,)), ((), ())),
                           preferred_element_type=_F32)


def _bf16_pieces(x):
    hi = x.astype(_BF16).astype(_F32)
    r1 = x - hi
    mid = r1.astype(_BF16).astype(_F32)
    return hi, mid, r1 - mid


def _inproj_kernel(x_ref, g_ref, w_ref, bf_ref, cw_ref, cg_ref, halo0_ref,
                   carry0_ref, qt_ref, kx_ref, vt_ref, z_ref, yc_ref, *rest,
                   tm, emit_tail):
    if emit_tail:
        cx_ref, crow_ref, halo_s, carry_s, cx_s = rest
    else:
        halo_s, carry_s, cx_s = rest
    i = pl.program_id(1)

    @pl.when(i == 0)
    def _():
        halo_s[...] = halo0_ref[...]
        carry_s[...] = jnp.broadcast_to(carry0_ref[...], (ATTN_HEADS, LANES))

    x = x_ref[0]
    ms = jnp.mean(x * x, axis=-1, keepdims=True)
    u = (x * lax.rsqrt(ms + EPS) * g_ref[...]).astype(_BF16)

    def proj(c):
        lo = QVF_ROWS + c * 512
        return _dot_nt(u, w_ref[lo:lo + 512, :])

    gate_b = proj(2)
    cx = proj(3) * proj(4)
    cx_s[0:SUBLANES, :] = halo_s[...]
    cx_s[SUBLANES:SUBLANES + tm, :] = cx
    if emit_tail:
        cx_ref[0] = cx
    conv = (cw_ref[0:1, :] * cx_s[SUBLANES - 2:SUBLANES - 2 + tm, :]
            + cw_ref[1:2, :] * cx_s[SUBLANES - 1:SUBLANES - 1 + tm, :]
            + cw_ref[2:3, :] * cx)
    halo_s[...] = cx[tm - SUBLANES:tm, :]
    yb = gate_b * conv
    zc = proj(5)
    low = lax.broadcasted_iota(jnp.int32, (1, LANES), 1) < HEAD_DIM
    for cb in range(D_CONV // LANES):
        sl = slice(cb * LANES, (cb + 1) * LANES)
        y = yb[:, sl]
        y2 = y * y
        s_lo = jnp.sum(jnp.where(low, y2, 0.0), axis=-1, keepdims=True)
        s_hi = jnp.sum(jnp.where(low, 0.0, y2), axis=-1, keepdims=True)
        msq = jnp.where(low, s_lo, s_hi) * (1.0 / HEAD_DIM)
        yn = y * lax.rsqrt(msq + EPS) * cg_ref[:, sl]
        yc_ref[0, :, sl] = (yn * _silu(zc[:, sl])).astype(_BF16)

    qvt = _dot_nt(w_ref[0:QVF_ROWS, :], u)
    qt_ref[0] = (qvt[:D_ATTN] * (LOG2E * HEAD_DIM ** -0.5)).astype(_BF16)
    vt_ref[0] = qvt[D_ATTN:2 * D_ATTN].astype(_BF16)

    logf = _log_sigmoid(qvt[2 * D_ATTN:2 * D_ATTN + ATTN_HEADS] + bf_ref[...])
    hi, mid, lo = _bf16_pieces(logf)
    pieces = jnp.concatenate([hi, mid, lo, jnp.zeros_like(hi)], axis=0).astype(_BF16)
    row = lax.broadcasted_iota(jnp.int32, (tm, tm), 0)
    col = lax.broadcasted_iota(jnp.int32, (tm, tm), 1)
    tri = jnp.where(row <= col, 1.0, 0.0).astype(_BF16)
    c3 = jnp.dot(pieces, tri, preferred_element_type=_F32)
    h8 = ATTN_HEADS
    csum = c3[0:h8] + c3[h8:2 * h8] + c3[2 * h8:3 * h8] + carry_s[:, 0:1]
    carry_s[...] = jnp.broadcast_to(csum[:, tm - 1:tm], (h8, LANES))
    if emit_tail:
        crow_ref[0] = csum
    hi, mid, lo = _bf16_pieces(csum * (-LOG2E))
    aug = jnp.concatenate(
        [jnp.zeros((HEAD_DIM, tm), _F32), hi, mid, lo,
         jnp.zeros((LANES - HEAD_DIM - N_PIECES * h8, tm), _F32)], axis=0).T

    k = proj(0)
    for p in range(D_ATTN // LANES):
        kp = k[:, p * LANES:(p + 1) * LANES]
        sl = slice(2 * p * LANES, (2 * p + 1) * LANES)
        kx_ref[0, :, sl] = jnp.where(low, kp, aug).astype(_BF16)
        sl = slice((2 * p + 1) * LANES, (2 * p + 2) * LANES)
        kx_ref[0, :, sl] = jnp.where(low, pltpu.roll(kp, HEAD_DIM, axis=1), aug).astype(_BF16)
    z_ref[0] = proj(1).astype(_BF16)


def _inproj(x3, g, w_t, bf, cw, cg, halo0, carry0, *, tm, emit_tail):
    b, l, d = x3.shape
    nt = l // tm
    kern = functools.partial(_inproj_kernel, tm=tm, emit_tail=emit_tail)
    const = lambda bi, i: (0, 0)
    row_blk = lambda bi, i: (bi, i, 0)
    col_blk = lambda bi, i: (bi, 0, i)
    rows = (jax.ShapeDtypeStruct((b, l, 512), _BF16), pl.BlockSpec((1, tm, 512), row_blk))
    cols = (jax.ShapeDtypeStruct((b, 512, l), _BF16), pl.BlockSpec((1, 512, tm), col_blk))
    keys = (jax.ShapeDtypeStruct((b, l, ATTN_HEADS * LANES), _BF16),
            pl.BlockSpec((1, tm, ATTN_HEADS * LANES), row_blk))
    out_shape, out_specs = (list(t) for t in zip(cols, keys, cols, rows, rows))
    if emit_tail:
        out_shape.append(jax.ShapeDtypeStruct((b, l, D_CONV), _F32))
        out_specs.append(pl.BlockSpec((1, tm, D_CONV), row_blk))
        out_shape.append(jax.ShapeDtypeStruct((b, ATTN_HEADS, l), _F32))
        out_specs.append(pl.BlockSpec((1, ATTN_HEADS, tm), col_blk))
    return pl.pallas_call(
        kern,
        grid=(b, nt),
        in_specs=[
            pl.BlockSpec((1, tm, d), row_blk),
            pl.BlockSpec((1, d), const),
            pl.BlockSpec(w_t.shape, const),
            pl.BlockSpec((ATTN_HEADS, 1), const),
            pl.BlockSpec((CONV_WIDTH, D_CONV), const),
            pl.BlockSpec((1, D_CONV), const),
            pl.BlockSpec((SUBLANES, D_CONV), const),
            pl.BlockSpec((ATTN_HEADS, 1), const),
        ],
        out_specs=out_specs,
        out_shape=out_shape,
        scratch_shapes=[
            pltpu.VMEM((SUBLANES, D_CONV), _F32),
            pltpu.VMEM((ATTN_HEADS, LANES), _F32),
            pltpu.VMEM((tm + SUBLANES, D_CONV), _F32),
        ],
        compiler_params=pltpu.CompilerParams(
            dimension_semantics=("arbitrary", "arbitrary"),
            vmem_limit_bytes=VMEM_LIMIT),
        name="inproj_meta" if emit_tail else "inproj",
    )(x3, g, w_t, bf, cw, cg, halo0, carry0)


def _attn_kernel(pq_ref, pk_ref, qt_ref, kx_ref, vt_ref, kxm_ref, vmt_ref,
                 z_ref, g_ref, o_ref, q2t_s, sd1_buf, sd2_buf, su_buf, mbd1_s, mbd2_s, mbu_s,
                 mask1_s, mask2_s, m_s, acc_s, *, tq, n_q, n_pairs):
    tk = tq
    th = tq // 2
    pair = pl.program_id(1)
    ones_v = jnp.ones((ONES_ROWS, tk), _BF16)
    ones_m = jnp.ones((ONES_ROWS, N_META), _BF16)

    @pl.when(jnp.logical_and(pl.program_id(0) == 0, pair == 0))
    def _():
        c = lax.broadcasted_iota(jnp.int32, (th, 2 * tq), 1)
        q_loc = jnp.where(c >= tq, c - tq, c)
        k_loc = lax.broadcasted_iota(jnp.int32, (th, 2 * tq), 0)
        mask1_s[0:th, :] = jnp.where(k_loc <= q_loc, 0.0, -jnp.inf)
        mask1_s[th:, :] = jnp.zeros((N_META, 2 * tq), _F32)
        c = lax.broadcasted_iota(jnp.int32, (th, 2 * th), 1)
        q_loc = jnp.where(c >= th, c - th, c)
        k_loc = lax.broadcasted_iota(jnp.int32, (th, 2 * th), 0)
        mask2_s[...] = jnp.where(k_loc <= q_loc, 0.0, -jnp.inf)

    def ones_for(h):
        r = lax.broadcasted_iota(jnp.int32, (HEAD_DIM, 1), 0)
        sel = jnp.logical_and(r % ATTN_HEADS == h, r < N_PIECES * ATTN_HEADS)
        return jnp.broadcast_to(jnp.where(sel, 1.0, 0.0).astype(_BF16), (HEAD_DIM, tq))

    def build_q(qi, carry):
        qt = qt_ref[0, :, pl.ds(pl.multiple_of(qi * tq, tq), tq)]
        q2t_s[qi, 0:HEAD_DIM, 0:tq] = qt[0:HEAD_DIM]
        q2t_s[qi, 0:HEAD_DIM, tq:] = qt[HEAD_DIM:]
        q2t_s[qi, HEAD_DIM:, 0:tq] = ones_for(2 * pair)
        q2t_s[qi, HEAD_DIM:, tq:] = ones_for(2 * pair + 1)
        return carry

    lax.fori_loop(0, n_q, build_q, 0)

    def keys(j):
        return kx_ref[0, pl.ds(pl.multiple_of(j * tk, tk), tk), :]

    def logits(kx, qa, qb):
        return jnp.concatenate(
            [jnp.dot(kx[:, :LANES], qa, preferred_element_type=_F32),
             jnp.dot(kx[:, LANES:], qb, preferred_element_type=_F32)], axis=1)

    def values(j):
        start = pl.multiple_of(j * tk, tk)
        return jnp.concatenate([vt_ref[0, :, pl.ds(start, tk)], ones_v], axis=0)

    def late(x):
        return jnp.concatenate([x[..., th:tq], x[..., tq + th:]], axis=-1)

    def produce_d1(qi, slot):
        kx = jnp.concatenate([kx_ref[0, qi * tk:qi * tk + th, :], kxm_ref[...]], axis=0)
        q2t = q2t_s[qi]
        s = logits(kx, q2t[:, :tq], q2t[:, tq:]) + mask1_s[...]
        sd1_buf[slot] = s
        mbd1_s[slot] = jnp.max(s, axis=0, keepdims=True)

    def consume_d1(qi, slot):
        m = mbd1_s[slot]
        p = jnp.exp2(sd1_buf[slot] - m).astype(_BF16)
        m_s[qi] = m
        vals = jnp.concatenate([vt_ref[0, :, qi * tk:qi * tk + th], ones_v[:, :th]], axis=0)
        acc_s[qi] = (jnp.dot(vals, p[0:th], preferred_element_type=_F32)
                     + jnp.dot(jnp.concatenate([vmt_ref[...], ones_m], axis=0), p[th:],
                               preferred_element_type=_F32))

    def produce_d2(qi, slot):
        kx = kx_ref[0, qi * tk + th:(qi + 1) * tk, :]
        q2t = q2t_s[qi]
        s = logits(kx, q2t[:, th:tq], q2t[:, tq + th:]) + mask2_s[...]
        sd2_buf[slot] = s
        mbd2_s[slot] = jnp.max(s, axis=0, keepdims=True)

    def consume_d2(qi, slot):
        m = late(m_s[qi])
        m_new = jnp.maximum(m, mbd2_s[slot])
        alpha = jnp.exp2(m - m_new)
        p = jnp.exp2(sd2_buf[slot] - m_new).astype(_BF16)
        vals = jnp.concatenate([vt_ref[0, :, qi * tk + th:(qi + 1) * tk], ones_v[:, :th]],
                               axis=0)
        acc = alpha * late(acc_s[qi]) + jnp.dot(vals, p, preferred_element_type=_F32)
        m_s[qi, :, th:tq] = m_new[:, :th]
        m_s[qi, :, tq + th:] = m_new[:, th:]
        acc_s[qi, :, th:tq] = acc[:, :th]
        acc_s[qi, :, tq + th:] = acc[:, th:]

    def produce(t, slot):
        q2t = q2t_s[pq_ref[t]]
        s = logits(keys(pk_ref[t]), q2t[:, :tq], q2t[:, tq:])
        su_buf[slot] = s
        mbu_s[slot] = jnp.max(s, axis=0, keepdims=True)

    def consume(t, slot):
        qi = pq_ref[t]
        m = m_s[qi]
        m_new = jnp.maximum(m, mbu_s[slot])
        alpha = jnp.exp2(m - m_new)
        p = jnp.exp2(su_buf[slot] - m_new).astype(_BF16)
        m_s[qi] = m_new
        acc_s[qi] = alpha * acc_s[qi] + jnp.dot(values(pk_ref[t]), p,
                                                preferred_element_type=_F32)

    def finalize(qi, carry):
        rows = pl.ds(pl.multiple_of(qi * tq, tq), tq)
        inv_l = 1.0 / acc_s[qi, LANES:LANES + 1, :]

        def head_norm(rows_h, cols_h):
            o = acc_s[qi, rows_h, cols_h] * inv_l[:, cols_h]
            msq = jnp.sum(o * o, axis=0, keepdims=True) * (1.0 / HEAD_DIM)
            return o * lax.rsqrt(msq + EPS)

        on = jnp.concatenate([head_norm(slice(0, HEAD_DIM), slice(0, tq)),
                              head_norm(slice(HEAD_DIM, LANES), slice(tq, 2 * tq))], axis=0)
        y = on.T * g_ref[...]
        z = z_ref[0, rows, :].astype(_F32)
        o_ref[0, rows, :] = (y * _silu(z)).astype(_BF16)
        return carry

    produce_d1(0, 0)
    for qi in range(n_q):
        produce_d2(qi, qi % 2)
        consume_d1(qi, qi % 2)
        if qi + 1 < n_q:
            produce_d1(qi + 1, (qi + 1) % 2)
        else:
            produce(0, 0)
        consume_d2(qi, qi % 2)

    def ticks(t0, n):
        for d in range(n):
            produce(t0 + d, (1 + d) % 2)
            consume(t0 + d - 1, d % 2)

    def tick_group(u, carry):
        ticks(TICKS_PER_TRIP * u + 1, TICKS_PER_TRIP)
        return carry

    n_trips = (n_pairs - 1) // TICKS_PER_TRIP
    lax.fori_loop(0, n_trips, tick_group, 0)
    ticks(TICKS_PER_TRIP * n_trips + 1, (n_pairs - 1) % TICKS_PER_TRIP)
    consume(n_pairs - 1, (n_pairs - 1) % 2)

    lax.fori_loop(0, n_q, finalize, 0)


def _attention(qt, kx, vt, kxm, vmt, z, g, *, tq):
    b, l, _ = z.shape
    n_pair = D_ATTN // LANES
    n_q = l // tq
    assert n_q % 2 == 0 and n_q >= 2
    pq = [qi for qi in range(n_q) for _ in range(qi)]
    pk = [j for qi in range(n_q) for j in range(qi)]
    n_pairs = len(pq)
    assert n_pairs % 2 == 0 and n_pairs >= 2
    kern = functools.partial(_attn_kernel, tq=tq, n_q=n_q, n_pairs=n_pairs)
    seq_rows = pl.BlockSpec((1, l, LANES), lambda bi, p, *_: (bi, 0, p))
    seq_cols = pl.BlockSpec((1, LANES, l), lambda bi, p, *_: (bi, p, 0))
    th = tq // 2
    grid_spec = pltpu.PrefetchScalarGridSpec(
        num_scalar_prefetch=2,
        grid=(b, n_pair),
        in_specs=[
            seq_cols,
            pl.BlockSpec((1, l, 2 * LANES), lambda bi, p, *_: (bi, 0, p)),
            seq_cols,
            pl.BlockSpec((N_META, 2 * LANES), lambda bi, p, *_: (0, p)),
            pl.BlockSpec((LANES, N_META), lambda bi, p, *_: (p, 0)),
            seq_rows,
            pl.BlockSpec((1, LANES), lambda bi, p, *_: (0, p)),
        ],
        out_specs=seq_rows,
        scratch_shapes=[
            pltpu.VMEM((n_q, LANES, 2 * tq), _BF16),
            pltpu.VMEM((2, th + N_META, 2 * tq), _F32),
            pltpu.VMEM((2, th, 2 * th), _F32),
            pltpu.VMEM((2, tq, 2 * tq), _F32),
            pltpu.VMEM((2, 1, 2 * tq), _F32),
            pltpu.VMEM((2, 1, 2 * th), _F32),
            pltpu.VMEM((2, 1, 2 * tq), _F32),
            pltpu.VMEM((th + N_META, 2 * tq), _F32),
            pltpu.VMEM((th, 2 * th), _F32),
            pltpu.VMEM((n_q, 1, 2 * tq), _F32),
            pltpu.VMEM((n_q, LANES + ONES_ROWS, 2 * tq), _F32),
        ],
    )
    return pl.pallas_call(
        kern,
        grid_spec=grid_spec,
        out_shape=jax.ShapeDtypeStruct((b, l, D_ATTN), _BF16),
        compiler_params=pltpu.CompilerParams(
            dimension_semantics=("arbitrary", "arbitrary"),
            vmem_limit_bytes=VMEM_LIMIT),
        name="fox_attention",
    )(jnp.asarray(pq, jnp.int32), jnp.asarray(pk, jnp.int32),
      qt, kx, vt, kxm, vmt, z, g)


X_SLOTS = 3


def _outproj_kernel(x_hbm, ya_ref, yc_ref, w_ref, g_ref, o_ref, x_buf, x_sem, *, tm, nt):
    step = pl.program_id(0) * nt + pl.program_id(1)
    n_steps = pl.num_programs(0) * nt

    def x_copy(s):
        slot = s % X_SLOTS
        rows = pl.ds(pl.multiple_of((s % nt) * tm, tm), tm)
        return pltpu.make_async_copy(x_hbm.at[s // nt, rows, :], x_buf.at[slot], x_sem.at[slot])

    @pl.when(step == 0)
    def _():
        for s in range(X_SLOTS - 1):
            x_copy(s).start()

    @pl.when(step + X_SLOTS - 1 < n_steps)
    def _():
        x_copy(step + X_SLOTS - 1).start()

    x_copy(step).wait()
    hres = (x_buf[step % X_SLOTS]
            + jnp.dot(ya_ref[0], w_ref[0:D_ATTN, :], preferred_element_type=_F32)
            + jnp.dot(yc_ref[0], w_ref[D_ATTN:, :], preferred_element_type=_F32))
    ms = jnp.mean(hres * hres, axis=-1, keepdims=True)
    o_ref[0] = hres * lax.rsqrt(ms + EPS) * g_ref[...]


def _outproj(x, ya, yc, w_out, g, *, tm):
    b, l, d = x.shape
    nt = l // tm
    assert b * nt >= X_SLOTS - 1
    row_blk = lambda bi, i: (bi, i, 0)
    const = lambda bi, i: (0, 0)
    return pl.pallas_call(
        functools.partial(_outproj_kernel, tm=tm, nt=nt),
        grid=(b, nt),
        in_specs=[
            pl.BlockSpec(memory_space=pl.ANY),
            pl.BlockSpec((1, tm, D_ATTN), row_blk),
            pl.BlockSpec((1, tm, D_CONV), row_blk),
            pl.BlockSpec(w_out.shape, const),
            pl.BlockSpec((1, d), const),
        ],
        out_specs=pl.BlockSpec((1, tm, d), row_blk),
        out_shape=jax.ShapeDtypeStruct((b, l, d), _F32),
        scratch_shapes=[pltpu.VMEM((X_SLOTS, tm, d), _F32),
                        pltpu.SemaphoreType.DMA((X_SLOTS,))],
        compiler_params=pltpu.CompilerParams(
            dimension_semantics=("arbitrary", "arbitrary"),
            vmem_limit_bytes=VMEM_LIMIT),
        name="outproj",
    )(x, ya, yc, w_out, g)


def kernel(x, meta, norm_g, w_in, b_f, conv_w, attn_norm_g, conv_norm_g, w_out, final_norm_g):
    assert norm_g.shape[0] == 1, "single layer only"
    b, seq, d = x.shape
    wt = w_in[0].T
    f0 = 3 * D_ATTN
    w_t = jnp.concatenate(
        [wt[:D_ATTN], wt[2 * D_ATTN:f0 + ATTN_HEADS], jnp.zeros((ATTN_HEADS, d), wt.dtype),
         wt[D_ATTN:2 * D_ATTN], wt[f0 + ATTN_HEADS:]], axis=0).astype(_BF16)
    bf = b_f[0][:, None]
    cw = conv_w[0]

    meta_rows = LANES
    meta_p = jnp.pad(meta.astype(x.dtype), ((0, meta_rows - N_META), (0, 0)))[None]
    zero_halo = jnp.zeros((SUBLANES, D_CONV), _F32)
    zero_carry = jnp.zeros((ATTN_HEADS, 1), _F32)
    _, kxm, vmt, _, _, cxm, crow = _inproj(
        meta_p, norm_g, w_t, bf, cw, conv_norm_g, zero_halo, zero_carry,
        tm=meta_rows, emit_tail=True)
    kxm = kxm[0, :N_META]
    vmt = vmt[0, :, :N_META]
    halo0 = cxm[0, N_META - SUBLANES:N_META]
    carry0 = crow[0, :, N_META - 1:N_META]

    qt, kx, vt, z, yc = _inproj(x, norm_g, w_t, bf, cw, conv_norm_g,
                                halo0, carry0, tm=1024, emit_tail=False)
    ya = _attention(qt, kx, vt, kxm, vmt, z, attn_norm_g, tq=512)
    return _outproj(x, ya, yc, w_out[0].astype(_BF16), final_norm_g[None, :], tm=1024)
```

```python
import functools
import math

import jax
import jax.numpy as jnp
from jax import lax
from jax.experimental import pallas as pl
from jax.experimental.pallas import tpu as pltpu

D_MODEL = 1024
N_META = 16
ATTN_HEADS = 8
HEAD_DIM = 64
D_ATTN = 512
D_CONV = 512
CONV_WIDTH = 3
EPS = 1e-6
LANES = 128
SUBLANES = 8
N_PIECES = 3
QVF_ROWS = 2 * D_ATTN + 16
ONES_ROWS = 16
TICKS_PER_GROUP = 8
LOG2E = math.log2(math.e)
VMEM_LIMIT = 48 * 1024 * 1024

_BF16 = jnp.bfloat16
_F32 = jnp.float32


def _log_sigmoid(x):
    return jnp.minimum(x, 0.0) - jnp.log1p(jnp.exp(-jnp.abs(x)))


def _silu(x):
    return x * (1.0 / (1.0 + jnp.exp(-x)))


def _dot_nt(a, b):
    return lax.dot_general(a, b, (((1,), (1,)), ((), ())),
                           preferred_element_type=_F32)


def _bf16_pieces(x):
    hi = x.astype(_BF16).astype(_F32)
    r1 = x - hi
    mid = r1.astype(_BF16).astype(_F32)
    return hi, mid, r1 - mid


def _inproj_kernel(x_ref, g_ref, w_ref, bf_ref, cw_ref, cg_ref, halo0_ref,
                   carry0_ref, qt_ref, kx_ref, vt_ref, z_ref, yc_ref, *rest,
                   tm, emit_tail):
    if emit_tail:
        cx_ref, crow_ref, halo_s, carry_s, cx_s = rest
    else:
        halo_s, carry_s, cx_s = rest
    i = pl.program_id(1)

    @pl.when(i == 0)
    def _():
        halo_s[...] = halo0_ref[...]
        carry_s[...] = jnp.broadcast_to(carry0_ref[...], (ATTN_HEADS, LANES))

    x = x_ref[0]
    ms = jnp.mean(x * x, axis=-1, keepdims=True)
    u = (x * lax.rsqrt(ms + EPS) * g_ref[...]).astype(_BF16)

    def proj(c):
        lo = QVF_ROWS + c * 512
        return _dot_nt(u, w_ref[lo:lo + 512, :])

    gate_b = proj(2)
    cx = proj(3) * proj(4)
    cx_s[0:SUBLANES, :] = halo_s[...]
    cx_s[SUBLANES:SUBLANES + tm, :] = cx
    if emit_tail:
        cx_ref[0] = cx
    conv = (cw_ref[0:1, :] * cx_s[SUBLANES - 2:SUBLANES - 2 + tm, :]
            + cw_ref[1:2, :] * cx_s[SUBLANES - 1:SUBLANES - 1 + tm, :]
            + cw_ref[2:3, :] * cx)
    halo_s[...] = cx[tm - SUBLANES:tm, :]
    yb = gate_b * conv
    zc = proj(5)
    low = lax.broadcasted_iota(jnp.int32, (1, LANES), 1) < HEAD_DIM
    for cb in range(D_CONV // LANES):
        sl = slice(cb * LANES, (cb + 1) * LANES)
        y = yb[:, sl]
        y2 = y * y
        s_lo = jnp.sum(jnp.where(low, y2, 0.0), axis=-1, keepdims=True)
        s_hi = jnp.sum(jnp.where(low, 0.0, y2), axis=-1, keepdims=True)
        msq = jnp.where(low, s_lo, s_hi) * (1.0 / HEAD_DIM)
        yn = y * lax.rsqrt(msq + EPS) * cg_ref[:, sl]
        yc_ref[0, :, sl] = (yn * _silu(zc[:, sl])).astype(_BF16)

    qvt = _dot_nt(w_ref[0:QVF_ROWS, :], u)
    qt_ref[0] = (qvt[:D_ATTN] * (LOG2E * HEAD_DIM ** -0.5)).astype(_BF16)
    vt_ref[0] = qvt[D_ATTN:2 * D_ATTN].astype(_BF16)

    logf = _log_sigmoid(qvt[2 * D_ATTN:2 * D_ATTN + ATTN_HEADS] + bf_ref[...])
    hi, mid, lo = _bf16_pieces(logf)
    pieces = jnp.concatenate([hi, mid, lo, jnp.zeros_like(hi)], axis=0).astype(_BF16)
    row = lax.broadcasted_iota(jnp.int32, (tm, tm), 0)
    col = lax.broadcasted_iota(jnp.int32, (tm, tm), 1)
    tri = jnp.where(row <= col, 1.0, 0.0).astype(_BF16)
    c3 = jnp.dot(pieces, tri, preferred_element_type=_F32)
    h8 = ATTN_HEADS
    csum = c3[0:h8] + c3[h8:2 * h8] + c3[2 * h8:3 * h8] + carry_s[:, 0:1]
    carry_s[...] = jnp.broadcast_to(csum[:, tm - 1:tm], (h8, LANES))
    if emit_tail:
        crow_ref[0] = csum
    hi, mid, lo = _bf16_pieces(csum * (-LOG2E))
    aug = jnp.concatenate(
        [jnp.zeros((HEAD_DIM, tm), _F32), hi, mid, lo,
         jnp.zeros((LANES - HEAD_DIM - N_PIECES * h8, tm), _F32)], axis=0).T

    k = proj(0)
    for p in range(D_ATTN // LANES):
        kp = k[:, p * LANES:(p + 1) * LANES]
        sl = slice(2 * p * LANES, (2 * p + 1) * LANES)
        kx_ref[0, :, sl] = jnp.where(low, kp, aug).astype(_BF16)
        sl = slice((2 * p + 1) * LANES, (2 * p + 2) * LANES)
        kx_ref[0, :, sl] = jnp.where(low, pltpu.roll(kp, HEAD_DIM, axis=1), aug).astype(_BF16)
    z_ref[0] = proj(1).astype(_BF16)


def _inproj(x3, g, w_t, bf, cw, cg, halo0, carry0, *, tm, emit_tail):
    b, l, d = x3.shape
    nt = l // tm
    kern = functools.partial(_inproj_kernel, tm=tm, emit_tail=emit_tail)
    const = lambda bi, i: (0, 0)
    row_blk = lambda bi, i: (bi, i, 0)
    col_blk = lambda bi, i: (bi, 0, i)
    rows = (jax.ShapeDtypeStruct((b, l, 512), _BF16), pl.BlockSpec((1, tm, 512), row_blk))
    cols = (jax.ShapeDtypeStruct((b, 512, l), _BF16), pl.BlockSpec((1, 512, tm), col_blk))
    keys = (jax.ShapeDtypeStruct((b, l, ATTN_HEADS * LANES), _BF16),
            pl.BlockSpec((1, tm, ATTN_HEADS * LANES), row_blk))
    out_shape, out_specs = (list(t) for t in zip(cols, keys, cols, rows, rows))
    if emit_tail:
        out_shape.append(jax.ShapeDtypeStruct((b, l, D_CONV), _F32))
        out_specs.append(pl.BlockSpec((1, tm, D_CONV), row_blk))
        out_shape.append(jax.ShapeDtypeStruct((b, ATTN_HEADS, l), _F32))
        out_specs.append(pl.BlockSpec((1, ATTN_HEADS, tm), col_blk))
    return pl.pallas_call(
        kern,
        grid=(b, nt),
        in_specs=[
            pl.BlockSpec((1, tm, d), row_blk),
            pl.BlockSpec((1, d), const),
            pl.BlockSpec(w_t.shape, const),
            pl.BlockSpec((ATTN_HEADS, 1), const),
            pl.BlockSpec((CONV_WIDTH, D_CONV), const),
            pl.BlockSpec((1, D_CONV), const),
            pl.BlockSpec((SUBLANES, D_CONV), const),
            pl.BlockSpec((ATTN_HEADS, 1), const),
        ],
        out_specs=out_specs,
        out_shape=out_shape,
        scratch_shapes=[
            pltpu.VMEM((SUBLANES, D_CONV), _F32),
            pltpu.VMEM((ATTN_HEADS, LANES), _F32),
            pltpu.VMEM((tm + SUBLANES, D_CONV), _F32),
        ],
        compiler_params=pltpu.CompilerParams(
            dimension_semantics=("arbitrary", "arbitrary"),
            vmem_limit_bytes=VMEM_LIMIT),
        name="inproj_meta" if emit_tail else "inproj",
    )(x3, g, w_t, bf, cw, cg, halo0, carry0)


def _attn_kernel(qt_ref, kx_ref, vt_ref, kxm_ref, vmt_ref,
                 z_ref, g_ref, o_ref, q2t_s, sd1_buf, sd2_buf, su_buf, mbd1_s, mbd2_s, mbu_s,
                 mask1_s, mask2_s, m_s, acc_s, *, tq, n_q, pairs):
    n_pairs = len(pairs)
    tk = tq
    th = tq // 2
    pair = pl.program_id(1)
    ones_v = jnp.ones((ONES_ROWS, tk), _BF16)
    ones_m = jnp.ones((ONES_ROWS, N_META), _BF16)

    @pl.when(jnp.logical_and(pl.program_id(0) == 0, pair == 0))
    def _():
        c = lax.broadcasted_iota(jnp.int32, (th, 2 * tq), 1)
        q_loc = jnp.where(c >= tq, c - tq, c)
        k_loc = lax.broadcasted_iota(jnp.int32, (th, 2 * tq), 0)
        mask1_s[0:th, :] = jnp.where(k_loc <= q_loc, 0.0, -jnp.inf)
        mask1_s[th:, :] = jnp.zeros((N_META, 2 * tq), _F32)
        c = lax.broadcasted_iota(jnp.int32, (th, 2 * th), 1)
        q_loc = jnp.where(c >= th, c - th, c)
        k_loc = lax.broadcasted_iota(jnp.int32, (th, 2 * th), 0)
        mask2_s[...] = jnp.where(k_loc <= q_loc, 0.0, -jnp.inf)

    def ones_for(h):
        r = lax.broadcasted_iota(jnp.int32, (HEAD_DIM, 1), 0)
        sel = jnp.logical_and(r % ATTN_HEADS == h, r < N_PIECES * ATTN_HEADS)
        return jnp.broadcast_to(jnp.where(sel, 1.0, 0.0).astype(_BF16), (HEAD_DIM, tq))

    def build_q(qi, carry):
        qt = qt_ref[0, :, pl.ds(pl.multiple_of(qi * tq, tq), tq)]
        q2t_s[qi, 0:HEAD_DIM, 0:tq] = qt[0:HEAD_DIM]
        q2t_s[qi, 0:HEAD_DIM, tq:] = qt[HEAD_DIM:]
        q2t_s[qi, HEAD_DIM:, 0:tq] = ones_for(2 * pair)
        q2t_s[qi, HEAD_DIM:, tq:] = ones_for(2 * pair + 1)
        return carry

    lax.fori_loop(0, n_q, build_q, 0)

    def keys(j):
        return kx_ref[0, j * tk:(j + 1) * tk, :]

    def logits(kx, qa, qb):
        return jnp.concatenate(
            [jnp.dot(kx[:, :LANES], qa, preferred_element_type=_F32),
             jnp.dot(kx[:, LANES:], qb, preferred_element_type=_F32)], axis=1)

    def values(j):
        return jnp.concatenate([vt_ref[0, :, j * tk:(j + 1) * tk], ones_v], axis=0)

    def late(x):
        return jnp.concatenate([x[..., th:tq], x[..., tq + th:]], axis=-1)

    def produce_d1(qi, slot):
        kx = jnp.concatenate([kx_ref[0, qi * tk:qi * tk + th, :], kxm_ref[...]], axis=0)
        q2t = q2t_s[qi]
        s = logits(kx, q2t[:, :tq], q2t[:, tq:]) + mask1_s[...]
        sd1_buf[slot] = s
        mbd1_s[slot] = jnp.max(s, axis=0, keepdims=True)

    def consume_d1(qi, slot):
        m = mbd1_s[slot]
        p = jnp.exp2(sd1_buf[slot] - m).astype(_BF16)
        m_s[qi] = m
        vals = jnp.concatenate([vt_ref[0, :, qi * tk:qi * tk + th], ones_v[:, :th]], axis=0)
        acc_s[qi] = (jnp.dot(vals, p[0:th], preferred_element_type=_F32)
                     + jnp.dot(jnp.concatenate([vmt_ref[...], ones_m], axis=0), p[th:],
                               preferred_element_type=_F32))

    def produce_d2(qi, slot):
        kx = kx_ref[0, qi * tk + th:(qi + 1) * tk, :]
        q2t = q2t_s[qi]
        s = logits(kx, q2t[:, th:tq], q2t[:, tq + th:]) + mask2_s[...]
        sd2_buf[slot] = s
        mbd2_s[slot] = jnp.max(s, axis=0, keepdims=True)

    def consume_d2(qi, slot):
        m = late(m_s[qi])
        m_new = jnp.maximum(m, mbd2_s[slot])
        alpha = jnp.exp2(m - m_new)
        p = jnp.exp2(sd2_buf[slot] - m_new).astype(_BF16)
        vals = jnp.concatenate([vt_ref[0, :, qi * tk + th:(qi + 1) * tk], ones_v[:, :th]],
                               axis=0)
        acc = alpha * late(acc_s[qi]) + jnp.dot(vals, p, preferred_element_type=_F32)
        m_s[qi, :, th:tq] = m_new[:, :th]
        m_s[qi, :, tq + th:] = m_new[:, th:]
        acc_s[qi, :, th:tq] = acc[:, :th]
        acc_s[qi, :, tq + th:] = acc[:, th:]

    def produce(t, slot):
        qi, j = pairs[t]
        q2t = q2t_s[qi]
        s = logits(keys(j), q2t[:, :tq], q2t[:, tq:])
        su_buf[slot] = s
        mbu_s[slot] = jnp.max(s, axis=0, keepdims=True)

    def consume(t, slot):
        qi, j = pairs[t]
        m = m_s[qi]
        m_new = jnp.maximum(m, mbu_s[slot])
        alpha = jnp.exp2(m - m_new)
        p = jnp.exp2(su_buf[slot] - m_new).astype(_BF16)
        m_s[qi] = m_new
        acc_s[qi] = alpha * acc_s[qi] + jnp.dot(values(j), p, preferred_element_type=_F32)

    def finalize(qi):
        rows = slice(qi * tq, (qi + 1) * tq)
        inv_l = 1.0 / acc_s[qi, LANES:LANES + 1, :]

        def head_norm(rows_h, cols_h):
            o = acc_s[qi, rows_h, cols_h] * inv_l[:, cols_h]
            msq = jnp.sum(o * o, axis=0, keepdims=True) * (1.0 / HEAD_DIM)
            return o * lax.rsqrt(msq + EPS)

        on = jnp.concatenate([head_norm(slice(0, HEAD_DIM), slice(0, tq)),
                              head_norm(slice(HEAD_DIM, LANES), slice(tq, 2 * tq))], axis=0)
        y = on.T * g_ref[...]
        z = z_ref[0, rows, :].astype(_F32)
        o_ref[0, rows, :] = (y * _silu(z)).astype(_BF16)

    produce_d1(0, 0)
    for qi in range(n_q):
        produce_d2(qi, qi % 2)
        consume_d1(qi, qi % 2)
        if qi + 1 < n_q:
            produce_d1(qi + 1, (qi + 1) % 2)
        else:
            produce(0, 0)
        consume_d2(qi, qi % 2)
    finalize(0)

    def tick_group(t0):
        for t in range(t0, min(t0 + TICKS_PER_GROUP, n_pairs + 1)):
            if t < n_pairs:
                produce(t, t % 2)
            consume(t - 1, (t - 1) % 2)
            qi, j = pairs[t - 1]
            if j == qi - 1:
                finalize(qi)

    for t0 in range(1, n_pairs + 1, TICKS_PER_GROUP):
        pl.when(pl.program_id(0) >= 0)(functools.partial(tick_group, t0))


def _attention(qt, kx, vt, kxm, vmt, z, g, *, tq):
    b, l, _ = z.shape
    n_pair = D_ATTN // LANES
    n_q = l // tq
    assert n_q % 2 == 0 and n_q >= 2
    pq = [qi for qi in range(n_q) for _ in range(qi)]
    pk = [j for qi in range(n_q) for j in range(qi)]
    n_pairs = len(pq)
    assert n_pairs % 2 == 0 and n_pairs >= 2
    kern = functools.partial(_attn_kernel, tq=tq, n_q=n_q, pairs=tuple(zip(pq, pk)))
    seq_rows = pl.BlockSpec((1, l, LANES), lambda bi, p: (bi, 0, p))
    seq_cols = pl.BlockSpec((1, LANES, l), lambda bi, p: (bi, p, 0))
    th = tq // 2
    return pl.pallas_call(
        kern,
        grid=(b, n_pair),
        in_specs=[
            seq_cols,
            pl.BlockSpec((1, l, 2 * LANES), lambda bi, p: (bi, 0, p)),
            seq_cols,
            pl.BlockSpec((N_META, 2 * LANES), lambda bi, p: (0, p)),
            pl.BlockSpec((LANES, N_META), lambda bi, p: (p, 0)),
            seq_rows,
            pl.BlockSpec((1, LANES), lambda bi, p: (0, p)),
        ],
        out_specs=seq_rows,
        scratch_shapes=[
            pltpu.VMEM((n_q, LANES, 2 * tq), _BF16),
            pltpu.VMEM((2, th + N_META, 2 * tq), _F32),
            pltpu.VMEM((2, th, 2 * th), _F32),
            pltpu.VMEM((2, tq, 2 * tq), _F32),
            pltpu.VMEM((2, 1, 2 * tq), _F32),
            pltpu.VMEM((2, 1, 2 * th), _F32),
            pltpu.VMEM((2, 1, 2 * tq), _F32),
            pltpu.VMEM((th + N_META, 2 * tq), _F32),
            pltpu.VMEM((th, 2 * th), _F32),
            pltpu.VMEM((n_q, 1, 2 * tq), _F32),
            pltpu.VMEM((n_q, LANES + ONES_ROWS, 2 * tq), _F32),
        ],
        out_shape=jax.ShapeDtypeStruct((b, l, D_ATTN), _BF16),
        compiler_params=pltpu.CompilerParams(
            dimension_semantics=("arbitrary", "arbitrary"),
            vmem_limit_bytes=VMEM_LIMIT),
        name="fox_attention",
    )(qt, kx, vt, kxm, vmt, z, g)


X_SLOTS = 3


def _outproj_kernel(x_hbm, ya_ref, yc_ref, w_ref, g_ref, o_ref, x_buf, x_sem, *, tm, nt):
    step = pl.program_id(0) * nt + pl.program_id(1)
    n_steps = pl.num_programs(0) * nt

    def x_copy(s):
        slot = s % X_SLOTS
        rows = pl.ds(pl.multiple_of((s % nt) * tm, tm), tm)
        return pltpu.make_async_copy(x_hbm.at[s // nt, rows, :], x_buf.at[slot], x_sem.at[slot])

    @pl.when(step == 0)
    def _():
        for s in range(X_SLOTS - 1):
            x_copy(s).start()

    @pl.when(step + X_SLOTS - 1 < n_steps)
    def _():
        x_copy(step + X_SLOTS - 1).start()

    x_copy(step).wait()
    hres = (x_buf[step % X_SLOTS]
            + jnp.dot(ya_ref[0], w_ref[0:D_ATTN, :], preferred_element_type=_F32)
            + jnp.dot(yc_ref[0], w_ref[D_ATTN:, :], preferred_element_type=_F32))
    ms = jnp.mean(hres * hres, axis=-1, keepdims=True)
    o_ref[0] = hres * lax.rsqrt(ms + EPS) * g_ref[...]


def _outproj(x, ya, yc, w_out, g, *, tm):
    b, l, d = x.shape
    nt = l // tm
    assert b * nt >= X_SLOTS - 1
    row_blk = lambda bi, i: (bi, i, 0)
    const = lambda bi, i: (0, 0)
    return pl.pallas_call(
        functools.partial(_outproj_kernel, tm=tm, nt=nt),
        grid=(b, nt),
        in_specs=[
            pl.BlockSpec(memory_space=pl.ANY),
            pl.BlockSpec((1, tm, D_ATTN), row_blk),
            pl.BlockSpec((1, tm, D_CONV), row_blk),
            pl.BlockSpec(w_out.shape, const),
            pl.BlockSpec((1, d), const),
        ],
        out_specs=pl.BlockSpec((1, tm, d), row_blk),
        out_shape=jax.ShapeDtypeStruct((b, l, d), _F32),
        scratch_shapes=[pltpu.VMEM((X_SLOTS, tm, d), _F32),
                        pltpu.SemaphoreType.DMA((X_SLOTS,))],
        compiler_params=pltpu.CompilerParams(
            dimension_semantics=("arbitrary", "arbitrary"),
            vmem_limit_bytes=VMEM_LIMIT),
        name="outproj",
    )(x, ya, yc, w_out, g)


def kernel(x, meta, norm_g, w_in, b_f, conv_w, attn_norm_g, conv_norm_g, w_out, final_norm_g):
    assert norm_g.shape[0] == 1, "single layer only"
    b, seq, d = x.shape
    wt = w_in[0].T
    f0 = 3 * D_ATTN
    w_t = jnp.concatenate(
        [wt[:D_ATTN], wt[2 * D_ATTN:f0 + ATTN_HEADS], jnp.zeros((ATTN_HEADS, d), wt.dtype),
         wt[D_ATTN:2 * D_ATTN], wt[f0 + ATTN_HEADS:]], axis=0).astype(_BF16)
    bf = b_f[0][:, None]
    cw = conv_w[0]

    meta_rows = LANES
    meta_p = jnp.pad(meta.astype(x.dtype), ((0, meta_rows - N_META), (0, 0)))[None]
    zero_halo = jnp.zeros((SUBLANES, D_CONV), _F32)
    zero_carry = jnp.zeros((ATTN_HEADS, 1), _F32)
    _, kxm, vmt, _, _, cxm, crow = _inproj(
        meta_p, norm_g, w_t, bf, cw, conv_norm_g, zero_halo, zero_carry,
        tm=meta_rows, emit_tail=True)
    kxm = kxm[0, :N_META]
    vmt = vmt[0, :, :N_META]
    halo0 = cxm[0, N_META - SUBLANES:N_META]
    carry0 = crow[0, :, N_META - 1:N_META]

    qt, kx, vt, z, yc = _inproj(x, norm_g, w_t, bf, cw, conv_norm_g,
                                halo0, carry0, tm=1024, emit_tail=False)
    ya = _attention(qt, kx, vt, kxm, vmt, z, attn_norm_g, tq=512)
    return _outproj(x, ya, yc, w_out[0].astype(_BF16), final_norm_g[None, :], tm=1024)
```

```python
import functools
import math

import jax
import jax.numpy as jnp
from jax import lax
from jax.experimental import pallas as pl
from jax.experimental.pallas import tpu as pltpu

D_MODEL = 1024
N_META = 16
ATTN_HEADS = 8
HEAD_DIM = 64
D_ATTN = 512
D_CONV = 512
CONV_WIDTH = 3
EPS = 1e-6
LANES = 128
SUBLANES = 8
BF16_ROWS = 16
N_PIECES = 3
QVF_ROWS = 2 * D_ATTN + BF16_ROWS
ONES_ROWS = BF16_ROWS
LOG2E = math.log2(math.e)
VMEM_LIMIT = 48 * 1024 * 1024
INPROJ_ROWS = 1024
ATTN_BLOCK = 512
OUTPROJ_ROWS = 1024

_BF16 = jnp.bfloat16
_F32 = jnp.float32


def _log_sigmoid(x):
    return jnp.minimum(x, 0.0) - jnp.log1p(jnp.exp(-jnp.abs(x)))


def _silu(x):
    return x * (1.0 / (1.0 + jnp.exp(-x)))


def _dot_nt(a, b):
    return lax.dot_general(a, b, (((1,), (1,)), ((), ())),
                           preferred_element_type=_F32)


def _bf16_pieces(x):
    hi = x.astype(_BF16).astype(_F32)
    r1 = x - hi
    mid = r1.astype(_BF16).astype(_F32)
    return hi, mid, r1 - mid


def _inproj_kernel(x_ref, g_ref, w_ref, bf_ref, cw_ref, cg_ref, halo0_ref,
                   carry0_ref, qt_ref, kx_ref, vt_ref, z_ref, yc_ref, *rest,
                   tm, emit_tail):
    if emit_tail:
        cx_ref, crow_ref, halo_s, carry_s, cx_s = rest
    else:
        halo_s, carry_s, cx_s = rest
    i = pl.program_id(1)

    @pl.when(i == 0)
    def _():
        halo_s[...] = halo0_ref[...]
        carry_s[...] = jnp.broadcast_to(carry0_ref[...], (ATTN_HEADS, LANES))

    x = x_ref[0]
    ms = jnp.mean(x * x, axis=-1, keepdims=True)
    u = (x * lax.rsqrt(ms + EPS) * g_ref[...]).astype(_BF16)

    def proj(c):
        lo = QVF_ROWS + c * 512
        return _dot_nt(u, w_ref[lo:lo + 512, :])

    gate_b = proj(2)
    cx = proj(3) * proj(4)
    cx_s[0:SUBLANES, :] = halo_s[...]
    cx_s[SUBLANES:SUBLANES + tm, :] = cx
    if emit_tail:
        cx_ref[0] = cx
    conv = (cw_ref[0:1, :] * cx_s[SUBLANES - 2:SUBLANES - 2 + tm, :]
            + cw_ref[1:2, :] * cx_s[SUBLANES - 1:SUBLANES - 1 + tm, :]
            + cw_ref[2:3, :] * cx)
    halo_s[...] = cx[tm - SUBLANES:tm, :]
    yb = gate_b * conv
    zc = proj(5)
    low = lax.broadcasted_iota(jnp.int32, (1, LANES), 1) < HEAD_DIM
    for cb in range(D_CONV // LANES):
        sl = slice(cb * LANES, (cb + 1) * LANES)
        y = yb[:, sl]
        y2 = y * y
        s_lo = jnp.sum(jnp.where(low, y2, 0.0), axis=-1, keepdims=True)
        s_hi = jnp.sum(jnp.where(low, 0.0, y2), axis=-1, keepdims=True)
        msq = jnp.where(low, s_lo, s_hi) * (1.0 / HEAD_DIM)
        yn = y * lax.rsqrt(msq + EPS) * cg_ref[:, sl]
        yc_ref[0, :, sl] = (yn * _silu(zc[:, sl])).astype(_BF16)

    qvt = _dot_nt(w_ref[0:QVF_ROWS, :], u)
    qt_ref[0] = (qvt[:D_ATTN] * (LOG2E * HEAD_DIM ** -0.5)).astype(_BF16)
    vt_ref[0] = qvt[D_ATTN:2 * D_ATTN].astype(_BF16)

    logf = _log_sigmoid(qvt[2 * D_ATTN:2 * D_ATTN + ATTN_HEADS] + bf_ref[...])
    hi, mid, lo = _bf16_pieces(logf)
    pieces = jnp.concatenate([hi, mid, lo, jnp.zeros_like(hi)], axis=0).astype(_BF16)
    row = lax.broadcasted_iota(jnp.int32, (tm, tm), 0)
    col = lax.broadcasted_iota(jnp.int32, (tm, tm), 1)
    tri = jnp.where(row <= col, 1.0, 0.0).astype(_BF16)
    c3 = jnp.dot(pieces, tri, preferred_element_type=_F32)
    h8 = ATTN_HEADS
    csum = c3[0:h8] + c3[h8:2 * h8] + c3[2 * h8:3 * h8] + carry_s[:, 0:1]
    carry_s[...] = jnp.broadcast_to(csum[:, tm - 1:tm], (h8, LANES))
    if emit_tail:
        crow_ref[0] = csum
    hi, mid, lo = _bf16_pieces(csum * (-LOG2E))
    aug = jnp.concatenate(
        [jnp.zeros((HEAD_DIM, tm), _F32), hi, mid, lo,
         jnp.zeros((LANES - HEAD_DIM - N_PIECES * h8, tm), _F32)], axis=0).T

    k = proj(0)
    for p in range(D_ATTN // LANES):
        kp = k[:, p * LANES:(p + 1) * LANES]
        sl = slice(2 * p * LANES, (2 * p + 1) * LANES)
        kx_ref[0, :, sl] = jnp.where(low, kp, aug).astype(_BF16)
        sl = slice((2 * p + 1) * LANES, (2 * p + 2) * LANES)
        kx_ref[0, :, sl] = jnp.where(low, pltpu.roll(kp, HEAD_DIM, axis=1), aug).astype(_BF16)
    z_ref[0] = proj(1).astype(_BF16)


def _inproj(x3, g, w_t, bf, cw, cg, halo0, carry0, *, tm, emit_tail):
    b, l, d = x3.shape
    nt = l // tm
    kern = functools.partial(_inproj_kernel, tm=tm, emit_tail=emit_tail)
    const = lambda bi, i: (0, 0)
    row_blk = lambda bi, i: (bi, i, 0)
    col_blk = lambda bi, i: (bi, 0, i)
    rows = (jax.ShapeDtypeStruct((b, l, 512), _BF16), pl.BlockSpec((1, tm, 512), row_blk))
    cols = (jax.ShapeDtypeStruct((b, 512, l), _BF16), pl.BlockSpec((1, 512, tm), col_blk))
    keys = (jax.ShapeDtypeStruct((b, l, ATTN_HEADS * LANES), _BF16),
            pl.BlockSpec((1, tm, ATTN_HEADS * LANES), row_blk))
    out_shape, out_specs = (list(t) for t in zip(cols, keys, cols, rows, rows))
    if emit_tail:
        out_shape.append(jax.ShapeDtypeStruct((b, l, D_CONV), _F32))
        out_specs.append(pl.BlockSpec((1, tm, D_CONV), row_blk))
        out_shape.append(jax.ShapeDtypeStruct((b, ATTN_HEADS, l), _F32))
        out_specs.append(pl.BlockSpec((1, ATTN_HEADS, tm), col_blk))
    return pl.pallas_call(
        kern,
        grid=(b, nt),
        in_specs=[
            pl.BlockSpec((1, tm, d), row_blk),
            pl.BlockSpec((1, d), const),
            pl.BlockSpec(w_t.shape, const),
            pl.BlockSpec((ATTN_HEADS, 1), const),
            pl.BlockSpec((CONV_WIDTH, D_CONV), const),
            pl.BlockSpec((1, D_CONV), const),
            pl.BlockSpec((SUBLANES, D_CONV), const),
            pl.BlockSpec((ATTN_HEADS, 1), const),
        ],
        out_specs=out_specs,
        out_shape=out_shape,
        scratch_shapes=[
            pltpu.VMEM((SUBLANES, D_CONV), _F32),
            pltpu.VMEM((ATTN_HEADS, LANES), _F32),
            pltpu.VMEM((tm + SUBLANES, D_CONV), _F32),
        ],
        compiler_params=pltpu.CompilerParams(
            dimension_semantics=("arbitrary", "arbitrary"),
            vmem_limit_bytes=VMEM_LIMIT),
        name="inproj_meta" if emit_tail else "inproj",
    )(x3, g, w_t, bf, cw, cg, halo0, carry0)


def _attn_kernel(qt_ref, kx_ref, vt_ref, kxm_ref, vmt_ref,
                 z_ref, g_ref, o_ref, q2t_s, sd1_buf, sd2_buf, su_buf, mbd1_s, mbd2_s, mbu_s,
                 mask1_s, mask2_s, m_s, acc_s, *, tq, n_q, pairs):
    n_pairs = len(pairs)
    tk = tq
    th = tq // 2
    pair = pl.program_id(1)
    ones_v = jnp.ones((ONES_ROWS, tk), _BF16)
    ones_m = jnp.ones((ONES_ROWS, N_META), _BF16)

    @pl.when(jnp.logical_and(pl.program_id(0) == 0, pair == 0))
    def _():
        c = lax.broadcasted_iota(jnp.int32, (th, 2 * tq), 1)
        q_loc = jnp.where(c >= tq, c - tq, c)
        k_loc = lax.broadcasted_iota(jnp.int32, (th, 2 * tq), 0)
        mask1_s[0:th, :] = jnp.where(k_loc <= q_loc, 0.0, -jnp.inf)
        mask1_s[th:, :] = jnp.zeros((N_META, 2 * tq), _F32)
        c = lax.broadcasted_iota(jnp.int32, (th, 2 * th), 1)
        q_loc = jnp.where(c >= th, c - th, c)
        k_loc = lax.broadcasted_iota(jnp.int32, (th, 2 * th), 0)
        mask2_s[...] = jnp.where(k_loc <= q_loc, 0.0, -jnp.inf)

    def ones_for(h):
        r = lax.broadcasted_iota(jnp.int32, (HEAD_DIM, 1), 0)
        sel = jnp.logical_and(r % ATTN_HEADS == h, r < N_PIECES * ATTN_HEADS)
        return jnp.broadcast_to(jnp.where(sel, 1.0, 0.0).astype(_BF16), (HEAD_DIM, tq))

    def build_q(qi, carry):
        qt = qt_ref[0, :, pl.ds(pl.multiple_of(qi * tq, tq), tq)]
        q2t_s[qi, 0:HEAD_DIM, 0:tq] = qt[0:HEAD_DIM]
        q2t_s[qi, 0:HEAD_DIM, tq:] = qt[HEAD_DIM:]
        q2t_s[qi, HEAD_DIM:, 0:tq] = ones_for(2 * pair)
        q2t_s[qi, HEAD_DIM:, tq:] = ones_for(2 * pair + 1)
        return carry

    lax.fori_loop(0, n_q, build_q, 0)

    def keys(j):
        return kx_ref[0, j * tk:(j + 1) * tk, :]

    def logits(kx, qa, qb):
        return jnp.concatenate(
            [jnp.dot(kx[:, :LANES], qa, preferred_element_type=_F32),
             jnp.dot(kx[:, LANES:], qb, preferred_element_type=_F32)], axis=1)

    def values(j):
        return jnp.concatenate([vt_ref[0, :, j * tk:(j + 1) * tk], ones_v], axis=0)

    def late(x):
        return jnp.concatenate([x[..., th:tq], x[..., tq + th:]], axis=-1)

    def produce_d1(qi, slot):
        kx = jnp.concatenate([kx_ref[0, qi * tk:qi * tk + th, :], kxm_ref[...]], axis=0)
        q2t = q2t_s[qi]
        s = logits(kx, q2t[:, :tq], q2t[:, tq:]) + mask1_s[...]
        sd1_buf[slot] = s
        mbd1_s[slot] = jnp.max(s, axis=0, keepdims=True)

    def consume_d1(qi, slot):
        m = mbd1_s[slot]
        p = jnp.exp2(sd1_buf[slot] - m).astype(_BF16)
        m_s[qi] = m
        vals = jnp.concatenate([vt_ref[0, :, qi * tk:qi * tk + th], ones_v[:, :th]], axis=0)
        acc_s[qi] = (jnp.dot(vals, p[0:th], preferred_element_type=_F32)
                     + jnp.dot(jnp.concatenate([vmt_ref[...], ones_m], axis=0), p[th:],
                               preferred_element_type=_F32))

    def produce_d2(qi, slot):
        kx = kx_ref[0, qi * tk + th:(qi + 1) * tk, :]
        q2t = q2t_s[qi]
        s = logits(kx, q2t[:, th:tq], q2t[:, tq + th:]) + mask2_s[...]
        sd2_buf[slot] = s
        mbd2_s[slot] = jnp.max(s, axis=0, keepdims=True)

    def consume_d2(qi, slot):
        m = late(m_s[qi])
        m_new = jnp.maximum(m, mbd2_s[slot])
        alpha = jnp.exp2(m - m_new)
        p = jnp.exp2(sd2_buf[slot] - m_new).astype(_BF16)
        vals = jnp.concatenate([vt_ref[0, :, qi * tk + th:(qi + 1) * tk], ones_v[:, :th]],
                               axis=0)
        acc = alpha * late(acc_s[qi]) + jnp.dot(vals, p, preferred_element_type=_F32)
        m_s[qi, :, th:tq] = m_new[:, :th]
        m_s[qi, :, tq + th:] = m_new[:, th:]
        acc_s[qi, :, th:tq] = acc[:, :th]
        acc_s[qi, :, tq + th:] = acc[:, th:]

    def produce(t, slot):
        qi, j = pairs[t]
        q2t = q2t_s[qi]
        s = logits(keys(j), q2t[:, :tq], q2t[:, tq:])
        su_buf[slot] = s
        mbu_s[slot] = jnp.max(s, axis=0, keepdims=True)

    def consume(t, slot):
        qi, j = pairs[t]
        m = m_s[qi]
        m_new = jnp.maximum(m, mbu_s[slot])
        alpha = jnp.exp2(m - m_new)
        p = jnp.exp2(su_buf[slot] - m_new).astype(_BF16)
        m_s[qi] = m_new
        acc_s[qi] = alpha * acc_s[qi] + jnp.dot(values(j), p, preferred_element_type=_F32)

    def finalize(qi):
        rows = slice(qi * tq, (qi + 1) * tq)
        inv_l = 1.0 / acc_s[qi, LANES:LANES + 1, :]

        def head_norm(rows_h, cols_h):
            o = acc_s[qi, rows_h, cols_h] * inv_l[:, cols_h]
            msq = jnp.sum(o * o, axis=0, keepdims=True) * (1.0 / HEAD_DIM)
            return o * lax.rsqrt(msq + EPS)

        on = jnp.concatenate([head_norm(slice(0, HEAD_DIM), slice(0, tq)),
                              head_norm(slice(HEAD_DIM, LANES), slice(tq, 2 * tq))], axis=0)
        y = on.T * g_ref[...]
        z = z_ref[0, rows, :].astype(_F32)
        o_ref[0, rows, :] = (y * _silu(z)).astype(_BF16)

    def diag_blocks(q_lo, q_hi):
        for qi in range(q_lo, q_hi):
            produce_d2(qi, qi % 2)
            consume_d1(qi, qi % 2)
            if qi + 1 < n_q:
                produce_d1(qi + 1, (qi + 1) % 2)
            else:
                produce(0, 0)
            consume_d2(qi, qi % 2)
            if qi == 0:
                finalize(0)

    produce_d1(0, 0)
    diag_blocks(0, n_q // 2)
    pl.when(pl.program_id(1) >= 0)(functools.partial(diag_blocks, n_q // 2, n_q))

    @pl.when(pl.program_id(0) >= 0)
    def _():
        for t in range(1, n_pairs + 1):
            if t < n_pairs:
                produce(t, t % 2)
            consume(t - 1, (t - 1) % 2)
            qi, j = pairs[t - 1]
            if j == qi - 1:
                finalize(qi)


def _attention(qt, kx, vt, kxm, vmt, z, g, *, tq):
    b, l, _ = z.shape
    n_pair = D_ATTN // LANES
    n_q = l // tq
    assert n_q >= 2 and l % tq == 0
    pairs = tuple((qi, j) for qi in range(n_q) for j in range(qi))
    kern = functools.partial(_attn_kernel, tq=tq, n_q=n_q, pairs=pairs)
    seq_rows = pl.BlockSpec((1, l, LANES), lambda bi, p: (bi, 0, p))
    seq_cols = pl.BlockSpec((1, LANES, l), lambda bi, p: (bi, p, 0))
    th = tq // 2
    return pl.pallas_call(
        kern,
        grid=(b, n_pair),
        in_specs=[
            seq_cols,
            pl.BlockSpec((1, l, 2 * LANES), lambda bi, p: (bi, 0, p)),
            seq_cols,
            pl.BlockSpec((N_META, 2 * LANES), lambda bi, p: (0, p)),
            pl.BlockSpec((LANES, N_META), lambda bi, p: (p, 0)),
            seq_rows,
            pl.BlockSpec((1, LANES), lambda bi, p: (0, p)),
        ],
        out_specs=seq_rows,
        scratch_shapes=[
            pltpu.VMEM((n_q, LANES, 2 * tq), _BF16),
            pltpu.VMEM((2, th + N_META, 2 * tq), _F32),
            pltpu.VMEM((2, th, 2 * th), _F32),
            pltpu.VMEM((2, tq, 2 * tq), _F32),
            pltpu.VMEM((2, 1, 2 * tq), _F32),
            pltpu.VMEM((2, 1, 2 * th), _F32),
            pltpu.VMEM((2, 1, 2 * tq), _F32),
            pltpu.VMEM((th + N_META, 2 * tq), _F32),
            pltpu.VMEM((th, 2 * th), _F32),
            pltpu.VMEM((n_q, 1, 2 * tq), _F32),
            pltpu.VMEM((n_q, LANES + ONES_ROWS, 2 * tq), _F32),
        ],
        out_shape=jax.ShapeDtypeStruct((b, l, D_ATTN), _BF16),
        compiler_params=pltpu.CompilerParams(
            dimension_semantics=("arbitrary", "arbitrary"),
            vmem_limit_bytes=VMEM_LIMIT),
        name="fox_attention",
    )(qt, kx, vt, kxm, vmt, z, g)


X_SLOTS = 3


def _outproj_kernel(x_hbm, ya_ref, yc_ref, w_ref, g_ref, o_ref, x_buf, x_sem, *, tm, nt):
    step = pl.program_id(0) * nt + pl.program_id(1)
    n_steps = pl.num_programs(0) * nt

    def x_copy(s):
        slot = s % X_SLOTS
        rows = pl.ds(pl.multiple_of((s % nt) * tm, tm), tm)
        return pltpu.make_async_copy(x_hbm.at[s // nt, rows, :], x_buf.at[slot], x_sem.at[slot])

    @pl.when(step == 0)
    def _():
        for s in range(X_SLOTS - 1):
            x_copy(s).start()

    @pl.when(step + X_SLOTS - 1 < n_steps)
    def _():
        x_copy(step + X_SLOTS - 1).start()

    x_copy(step).wait()
    hres = (x_buf[step % X_SLOTS]
            + jnp.dot(ya_ref[0], w_ref[0:D_ATTN, :], preferred_element_type=_F32)
            + jnp.dot(yc_ref[0], w_ref[D_ATTN:, :], preferred_element_type=_F32))
    ms = jnp.mean(hres * hres, axis=-1, keepdims=True)
    o_ref[0] = hres * lax.rsqrt(ms + EPS) * g_ref[...]


def _outproj(x, ya, yc, w_out, g, *, tm):
    b, l, d = x.shape
    nt = l // tm
    assert b * nt >= X_SLOTS - 1
    row_blk = lambda bi, i: (bi, i, 0)
    const = lambda bi, i: (0, 0)
    return pl.pallas_call(
        functools.partial(_outproj_kernel, tm=tm, nt=nt),
        grid=(b, nt),
        in_specs=[
            pl.BlockSpec(memory_space=pl.ANY),
            pl.BlockSpec((1, tm, D_ATTN), row_blk),
            pl.BlockSpec((1, tm, D_CONV), row_blk),
            pl.BlockSpec(w_out.shape, const),
            pl.BlockSpec((1, d), const),
        ],
        out_specs=pl.BlockSpec((1, tm, d), row_blk),
        out_shape=jax.ShapeDtypeStruct((b, l, d), _F32),
        scratch_shapes=[pltpu.VMEM((X_SLOTS, tm, d), _F32),
                        pltpu.SemaphoreType.DMA((X_SLOTS,))],
        compiler_params=pltpu.CompilerParams(
            dimension_semantics=("arbitrary", "arbitrary"),
            vmem_limit_bytes=VMEM_LIMIT),
        name="outproj",
    )(x, ya, yc, w_out, g)


def kernel(x, meta, norm_g, w_in, b_f, conv_w, attn_norm_g, conv_norm_g, w_out, final_norm_g):
    assert norm_g.shape[0] == 1, "single layer only"
    b, seq, d = x.shape
    wt = w_in[0].T
    f0 = 3 * D_ATTN
    w_t = jnp.concatenate(
        [wt[:D_ATTN], wt[2 * D_ATTN:f0 + ATTN_HEADS], jnp.zeros((ATTN_HEADS, d), wt.dtype),
         wt[D_ATTN:2 * D_ATTN], wt[f0 + ATTN_HEADS:]], axis=0).astype(_BF16)
    bf = b_f[0][:, None]
    cw = conv_w[0]

    meta_rows = LANES
    meta_p = jnp.pad(meta.astype(x.dtype), ((0, meta_rows - N_META), (0, 0)))[None]
    zero_halo = jnp.zeros((SUBLANES, D_CONV), _F32)
    zero_carry = jnp.zeros((ATTN_HEADS, 1), _F32)
    _, kxm, vmt, _, _, cxm, crow = _inproj(
        meta_p, norm_g, w_t, bf, cw, conv_norm_g, zero_halo, zero_carry,
        tm=meta_rows, emit_tail=True)
    kxm = kxm[0, :N_META]
    vmt = vmt[0, :, :N_META]
    halo0 = cxm[0, N_META - SUBLANES:N_META]
    carry0 = crow[0, :, N_META - 1:N_META]

    qt, kx, vt, z, yc = _inproj(x, norm_g, w_t, bf, cw, conv_norm_g,
                                halo0, carry0, tm=INPROJ_ROWS, emit_tail=False)
    ya = _attention(qt, kx, vt, kxm, vmt, z, attn_norm_g, tq=ATTN_BLOCK)
    return _outproj(x, ya, yc, w_out[0].astype(_BF16), final_norm_g[None, :],
                    tm=OUTPROJ_ROWS)
```

```python
import functools
import math

import jax
import jax.numpy as jnp
from jax import lax
from jax.experimental import pallas as pl
from jax.experimental.pallas import tpu as pltpu

D_MODEL = 1024
N_META = 16
ATTN_HEADS = 8
HEAD_DIM = 64
D_ATTN = 512
D_CONV = 512
CONV_WIDTH = 3
EPS = 1e-6
LANES = 128
SUBLANES = 8
BF16_ROWS = 16
N_PIECES = 3
QVF_ROWS = 2 * D_ATTN + BF16_ROWS
ONES_ROWS = BF16_ROWS
LOG2E = math.log2(math.e)
VMEM_LIMIT = 48 * 1024 * 1024
INPROJ_ROWS = 1024
ATTN_BLOCK = 512
OUTPROJ_ROWS = 1024

_BF16 = jnp.bfloat16
_F32 = jnp.float32


def _log_sigmoid(x):
    return jnp.minimum(x, 0.0) - jnp.log1p(jnp.exp(-jnp.abs(x)))


def _silu(x):
    return x * (1.0 / (1.0 + jnp.exp(-x)))


def _dot_nt(a, b):
    return lax.dot_general(a, b, (((1,), (1,)), ((), ())),
                           preferred_element_type=_F32)


def _bf16_pieces(x):
    hi = x.astype(_BF16).astype(_F32)
    r1 = x - hi
    mid = r1.astype(_BF16).astype(_F32)
    return hi, mid, r1 - mid


def _inproj_kernel(x_ref, g_ref, w_ref, bf_ref, cw_ref, cg_ref, halo0_ref,
                   carry0_ref, qt_ref, kx_ref, vt_ref, z_ref, yc_ref, *rest,
                   tm, emit_tail):
    if emit_tail:
        cx_ref, crow_ref, carry_s, cx_s = rest
    else:
        carry_s, cx_s = rest
    i = pl.program_id(1)

    @pl.when(i == 0)
    def _():
        for d in (1, 2):
            cx_s[d - 1, SUBLANES:SUBLANES + d, :] = halo0_ref[SUBLANES - d:SUBLANES, :]
        carry_s[...] = jnp.broadcast_to(carry0_ref[...], (ATTN_HEADS, LANES))

    @pl.when(i > 0)
    def _():
        for d in (1, 2):
            cx_s[d - 1, SUBLANES:SUBLANES + d, :] = cx_s[d - 1, SUBLANES + tm:SUBLANES + tm + d, :]

    x = x_ref[0]
    ms = jnp.mean(x * x, axis=-1, keepdims=True)
    u = (x * lax.rsqrt(ms + EPS) * g_ref[...]).astype(_BF16)

    def proj(c):
        lo = QVF_ROWS + c * 512
        return _dot_nt(u, w_ref[lo:lo + 512, :])

    gate_b = proj(2)
    cx = proj(3) * proj(4)
    for d in (1, 2):
        cx_s[d - 1, SUBLANES + d:SUBLANES + d + tm, :] = cx
    if emit_tail:
        cx_ref[0] = cx
    conv = (cw_ref[0:1, :] * cx_s[1, SUBLANES:SUBLANES + tm, :]
            + cw_ref[1:2, :] * cx_s[0, SUBLANES:SUBLANES + tm, :]
            + cw_ref[2:3, :] * cx)
    yb = gate_b * conv
    zc = proj(5)
    low = lax.broadcasted_iota(jnp.int32, (1, LANES), 1) < HEAD_DIM
    for cb in range(D_CONV // LANES):
        sl = slice(cb * LANES, (cb + 1) * LANES)
        y = yb[:, sl]
        y2 = y * y
        s_lo = jnp.sum(jnp.where(low, y2, 0.0), axis=-1, keepdims=True)
        s_hi = jnp.sum(jnp.where(low, 0.0, y2), axis=-1, keepdims=True)
        msq = jnp.where(low, s_lo, s_hi) * (1.0 / HEAD_DIM)
        yn = y * lax.rsqrt(msq + EPS) * cg_ref[:, sl]
        yc_ref[0, :, sl] = (yn * _silu(zc[:, sl])).astype(_BF16)

    qvt = _dot_nt(w_ref[0:QVF_ROWS, :], u)
    qt_ref[0] = (qvt[:D_ATTN] * (LOG2E * HEAD_DIM ** -0.5)).astype(_BF16)
    vt_ref[0] = qvt[D_ATTN:2 * D_ATTN].astype(_BF16)

    logf = _log_sigmoid(qvt[2 * D_ATTN:2 * D_ATTN + ATTN_HEADS] + bf_ref[...])
    hi, mid, lo = _bf16_pieces(logf)
    pieces = jnp.concatenate([hi, mid, lo, jnp.zeros_like(hi)], axis=0).astype(_BF16)
    row = lax.broadcasted_iota(jnp.int32, (tm, tm), 0)
    col = lax.broadcasted_iota(jnp.int32, (tm, tm), 1)
    tri = jnp.where(row <= col, 1.0, 0.0).astype(_BF16)
    c3 = jnp.dot(pieces, tri, preferred_element_type=_F32)
    h8 = ATTN_HEADS
    csum = c3[0:h8] + c3[h8:2 * h8] + c3[2 * h8:3 * h8] + carry_s[:, 0:1]
    carry_s[...] = jnp.broadcast_to(csum[:, tm - 1:tm], (h8, LANES))
    if emit_tail:
        crow_ref[0] = csum
    hi, mid, lo = _bf16_pieces(csum * (-LOG2E))
    aug = jnp.concatenate(
        [jnp.zeros((HEAD_DIM, tm), _F32), hi, mid, lo,
         jnp.zeros((LANES - HEAD_DIM - N_PIECES * h8, tm), _F32)], axis=0).T

    k = proj(0)
    for p in range(D_ATTN // LANES):
        kp = k[:, p * LANES:(p + 1) * LANES]
        sl = slice(2 * p * LANES, (2 * p + 1) * LANES)
        kx_ref[0, :, sl] = jnp.where(low, kp, aug).astype(_BF16)
        sl = slice((2 * p + 1) * LANES, (2 * p + 2) * LANES)
        kx_ref[0, :, sl] = jnp.where(low, pltpu.roll(kp, HEAD_DIM, axis=1), aug).astype(_BF16)
    z_ref[0] = proj(1).astype(_BF16)


def _inproj(x3, g, w_t, bf, cw, cg, halo0, carry0, *, tm, emit_tail):
    b, l, d = x3.shape
    nt = l // tm
    kern = functools.partial(_inproj_kernel, tm=tm, emit_tail=emit_tail)
    const = lambda bi, i: (0, 0)
    row_blk = lambda bi, i: (bi, i, 0)
    col_blk = lambda bi, i: (bi, 0, i)
    rows = (jax.ShapeDtypeStruct((b, l, 512), _BF16), pl.BlockSpec((1, tm, 512), row_blk))
    cols = (jax.ShapeDtypeStruct((b, 512, l), _BF16), pl.BlockSpec((1, 512, tm), col_blk))
    keys = (jax.ShapeDtypeStruct((b, l, ATTN_HEADS * LANES), _BF16),
            pl.BlockSpec((1, tm, ATTN_HEADS * LANES), row_blk))
    out_shape, out_specs = (list(t) for t in zip(cols, keys, cols, rows, rows))
    if emit_tail:
        out_shape.append(jax.ShapeDtypeStruct((b, l, D_CONV), _F32))
        out_specs.append(pl.BlockSpec((1, tm, D_CONV), row_blk))
        out_shape.append(jax.ShapeDtypeStruct((b, ATTN_HEADS, l), _F32))
        out_specs.append(pl.BlockSpec((1, ATTN_HEADS, tm), col_blk))
    return pl.pallas_call(
        kern,
        grid=(b, nt),
        in_specs=[
            pl.BlockSpec((1, tm, d), row_blk),
            pl.BlockSpec((1, d), const),
            pl.BlockSpec(w_t.shape, const),
            pl.BlockSpec((ATTN_HEADS, 1), const),
            pl.BlockSpec((CONV_WIDTH, D_CONV), const),
            pl.BlockSpec((1, D_CONV), const),
            pl.BlockSpec((SUBLANES, D_CONV), const),
            pl.BlockSpec((ATTN_HEADS, 1), const),
        ],
        out_specs=out_specs,
        out_shape=out_shape,
        scratch_shapes=[
            pltpu.VMEM((ATTN_HEADS, LANES), _F32),
            pltpu.VMEM((2, tm + 2 * SUBLANES, D_CONV), _F32),
        ],
        compiler_params=pltpu.CompilerParams(
            dimension_semantics=("arbitrary", "arbitrary"),
            vmem_limit_bytes=VMEM_LIMIT),
        name="inproj_meta" if emit_tail else "inproj",
    )(x3, g, w_t, bf, cw, cg, halo0, carry0)


def _attn_kernel(qt_ref, kx_ref, vt_ref, kxm_ref, vmt_ref,
                 z_ref, g_ref, o_ref, q2t_s, sd1_buf, sd2_buf, su_buf, mbd1_s, mbd2_s, mbu_s,
                 mask1_s, mask2_s, m_s, acc_s, *, tq, n_q, pairs):
    n_pairs = len(pairs)
    tk = tq
    th = tq // 2
    pair = pl.program_id(1)
    ones_v = jnp.ones((ONES_ROWS, tk), _BF16)
    ones_m = jnp.ones((ONES_ROWS, N_META), _BF16)

    @pl.when(jnp.logical_and(pl.program_id(0) == 0, pair == 0))
    def _():
        c = lax.broadcasted_iota(jnp.int32, (th, 2 * tq), 1)
        q_loc = jnp.where(c >= tq, c - tq, c)
        k_loc = lax.broadcasted_iota(jnp.int32, (th, 2 * tq), 0)
        mask1_s[0:th, :] = jnp.where(k_loc <= q_loc, 0.0, -jnp.inf)
        mask1_s[th:, :] = jnp.zeros((N_META, 2 * tq), _F32)
        c = lax.broadcasted_iota(jnp.int32, (th, 2 * th), 1)
        q_loc = jnp.where(c >= th, c - th, c)
        k_loc = lax.broadcasted_iota(jnp.int32, (th, 2 * th), 0)
        mask2_s[...] = jnp.where(k_loc <= q_loc, 0.0, -jnp.inf)

    def ones_for(h):
        r = lax.broadcasted_iota(jnp.int32, (HEAD_DIM, 1), 0)
        sel = jnp.logical_and(r % ATTN_HEADS == h, r < N_PIECES * ATTN_HEADS)
        return jnp.broadcast_to(jnp.where(sel, 1.0, 0.0).astype(_BF16), (HEAD_DIM, tq))

    def build_q(qi, carry):
        qt = qt_ref[0, :, pl.ds(pl.multiple_of(qi * tq, tq), tq)]
        q2t_s[qi, 0:HEAD_DIM, 0:tq] = qt[0:HEAD_DIM]
        q2t_s[qi, 0:HEAD_DIM, tq:] = qt[HEAD_DIM:]
        q2t_s[qi, HEAD_DIM:, 0:tq] = ones_for(2 * pair)
        q2t_s[qi, HEAD_DIM:, tq:] = ones_for(2 * pair + 1)
        return carry

    lax.fori_loop(0, n_q, build_q, 0)

    def keys(j):
        return kx_ref[0, j * tk:(j + 1) * tk, :]

    def logits(kx, qa, qb):
        return jnp.concatenate(
            [jnp.dot(kx[:, :LANES], qa, preferred_element_type=_F32),
             jnp.dot(kx[:, LANES:], qb, preferred_element_type=_F32)], axis=1)

    def values(j):
        return jnp.concatenate([vt_ref[0, :, j * tk:(j + 1) * tk], ones_v], axis=0)

    def late(x):
        return jnp.concatenate([x[..., th:tq], x[..., tq + th:]], axis=-1)

    def produce_d1(qi, slot):
        kx = jnp.concatenate([kx_ref[0, qi * tk:qi * tk + th, :], kxm_ref[...]], axis=0)
        q2t = q2t_s[qi]
        s = logits(kx, q2t[:, :tq], q2t[:, tq:]) + mask1_s[...]
        sd1_buf[slot] = s
        mbd1_s[slot] = jnp.max(s, axis=0, keepdims=True)

    def consume_d1(qi, slot):
        m = mbd1_s[slot]
        p = jnp.exp2(sd1_buf[slot] - m).astype(_BF16)
        m_s[qi] = m
        vals = jnp.concatenate([vt_ref[0, :, qi * tk:qi * tk + th], ones_v[:, :th]], axis=0)
        acc_s[qi] = (jnp.dot(vals, p[0:th], preferred_element_type=_F32)
                     + jnp.dot(jnp.concatenate([vmt_ref[...], ones_m], axis=0), p[th:],
                               preferred_element_type=_F32))

    def produce_d2(qi, slot):
        kx = kx_ref[0, qi * tk + th:(qi + 1) * tk, :]
        q2t = q2t_s[qi]
        s = logits(kx, q2t[:, th:tq], q2t[:, tq + th:]) + mask2_s[...]
        sd2_buf[slot] = s
        mbd2_s[slot] = jnp.max(s, axis=0, keepdims=True)

    def consume_d2(qi, slot):
        m = late(m_s[qi])
        m_new = jnp.maximum(m, mbd2_s[slot])
        alpha = jnp.exp2(m - m_new)
        p = jnp.exp2(sd2_buf[slot] - m_new).astype(_BF16)
        vals = jnp.concatenate([vt_ref[0, :, qi * tk + th:(qi + 1) * tk], ones_v[:, :th]],
                               axis=0)
        acc = alpha * late(acc_s[qi]) + jnp.dot(vals, p, preferred_element_type=_F32)
        m_s[qi, :, th:tq] = m_new[:, :th]
        m_s[qi, :, tq + th:] = m_new[:, th:]
        acc_s[qi, :, th:tq] = acc[:, :th]
        acc_s[qi, :, tq + th:] = acc[:, th:]

    def produce(t, slot):
        qi, j = pairs[t]
        q2t = q2t_s[qi]
        s = logits(keys(j), q2t[:, :tq], q2t[:, tq:])
        su_buf[slot] = s
        mbu_s[slot] = jnp.max(s, axis=0, keepdims=True)

    def consume(t, slot):
        qi, j = pairs[t]
        m = m_s[qi]
        m_new = jnp.maximum(m, mbu_s[slot])
        alpha = jnp.exp2(m - m_new)
        p = jnp.exp2(su_buf[slot] - m_new).astype(_BF16)
        m_s[qi] = m_new
        acc_s[qi] = alpha * acc_s[qi] + jnp.dot(values(j), p, preferred_element_type=_F32)

    def finalize(qi):
        rows = slice(qi * tq, (qi + 1) * tq)
        inv_l = 1.0 / acc_s[qi, LANES:LANES + 1, :]

        def head_norm(rows_h, cols_h):
            o = acc_s[qi, rows_h, cols_h] * inv_l[:, cols_h]
            msq = jnp.sum(o * o, axis=0, keepdims=True) * (1.0 / HEAD_DIM)
            return o * lax.rsqrt(msq + EPS)

        on = jnp.concatenate([head_norm(slice(0, HEAD_DIM), slice(0, tq)),
                              head_norm(slice(HEAD_DIM, LANES), slice(tq, 2 * tq))], axis=0)
        y = on.T * g_ref[...]
        z = z_ref[0, rows, :].astype(_F32)
        o_ref[0, rows, :] = (y * _silu(z)).astype(_BF16)

    def diag_blocks(q_lo, q_hi):
        for qi in range(q_lo, q_hi):
            produce_d2(qi, qi % 2)
            consume_d1(qi, qi % 2)
            if qi + 1 < n_q:
                produce_d1(qi + 1, (qi + 1) % 2)
            else:
                produce(0, 0)
            consume_d2(qi, qi % 2)
            if qi == 0:
                finalize(0)

    produce_d1(0, 0)
    diag_blocks(0, n_q // 2)
    pl.when(pl.program_id(1) >= 0)(functools.partial(diag_blocks, n_q // 2, n_q))

    @pl.when(pl.program_id(0) >= 0)
    def _():
        for t in range(1, n_pairs + 1):
            if t < n_pairs:
                produce(t, t % 2)
            consume(t - 1, (t - 1) % 2)
            qi, j = pairs[t - 1]
            if j == qi - 1:
                finalize(qi)


def _attention(qt, kx, vt, kxm, vmt, z, g, *, tq):
    b, l, _ = z.shape
    n_pair = D_ATTN // LANES
    n_q = l // tq
    assert n_q >= 2 and l % tq == 0
    pairs = tuple((qi, j) for qi in range(n_q) for j in range(qi))
    kern = functools.partial(_attn_kernel, tq=tq, n_q=n_q, pairs=pairs)
    seq_rows = pl.BlockSpec((1, l, LANES), lambda bi, p: (bi, 0, p))
    seq_cols = pl.BlockSpec((1, LANES, l), lambda bi, p: (bi, p, 0))
    th = tq // 2
    return pl.pallas_call(
        kern,
        grid=(b, n_pair),
        in_specs=[
            seq_cols,
            pl.BlockSpec((1, l, 2 * LANES), lambda bi, p: (bi, 0, p)),
            seq_cols,
            pl.BlockSpec((N_META, 2 * LANES), lambda bi, p: (0, p)),
            pl.BlockSpec((LANES, N_META), lambda bi, p: (p, 0)),
            seq_rows,
            pl.BlockSpec((1, LANES), lambda bi, p: (0, p)),
        ],
        out_specs=seq_rows,
        scratch_shapes=[
            pltpu.VMEM((n_q, LANES, 2 * tq), _BF16),
            pltpu.VMEM((2, th + N_META, 2 * tq), _F32),
            pltpu.VMEM((2, th, 2 * th), _F32),
            pltpu.VMEM((2, tq, 2 * tq), _F32),
            pltpu.VMEM((2, 1, 2 * tq), _F32),
            pltpu.VMEM((2, 1, 2 * th), _F32),
            pltpu.VMEM((2, 1, 2 * tq), _F32),
            pltpu.VMEM((th + N_META, 2 * tq), _F32),
            pltpu.VMEM((th, 2 * th), _F32),
            pltpu.VMEM((n_q, 1, 2 * tq), _F32),
            pltpu.VMEM((n_q, LANES + ONES_ROWS, 2 * tq), _F32),
        ],
        out_shape=jax.ShapeDtypeStruct((b, l, D_ATTN), _BF16),
        compiler_params=pltpu.CompilerParams(
            dimension_semantics=("arbitrary", "arbitrary"),
            vmem_limit_bytes=VMEM_LIMIT),
        name="fox_attention",
    )(qt, kx, vt, kxm, vmt, z, g)


X_SLOTS = 3


def _outproj_kernel(x_hbm, ya_ref, yc_ref, w_ref, g_ref, o_ref, x_buf, x_sem, *, tm, nt):
    step = pl.program_id(0) * nt + pl.program_id(1)
    n_steps = pl.num_programs(0) * nt

    def x_copy(s):
        slot = s % X_SLOTS
        rows = pl.ds(pl.multiple_of((s % nt) * tm, tm), tm)
        return pltpu.make_async_copy(x_hbm.at[s // nt, rows, :], x_buf.at[slot], x_sem.at[slot])

    @pl.when(step == 0)
    def _():
        for s in range(X_SLOTS - 1):
            x_copy(s).start()

    @pl.when(step + X_SLOTS - 1 < n_steps)
    def _():
        x_copy(step + X_SLOTS - 1).start()

    x_copy(step).wait()
    hres = (x_buf[step % X_SLOTS]
            + jnp.dot(ya_ref[0], w_ref[0:D_ATTN, :], preferred_element_type=_F32)
            + jnp.dot(yc_ref[0], w_ref[D_ATTN:, :], preferred_element_type=_F32))
    ms = jnp.mean(hres * hres, axis=-1, keepdims=True)
    o_ref[0] = hres * lax.rsqrt(ms + EPS) * g_ref[...]


def _outproj(x, ya, yc, w_out, g, *, tm):
    b, l, d = x.shape
    nt = l // tm
    assert b * nt >= X_SLOTS - 1
    row_blk = lambda bi, i: (bi, i, 0)
    const = lambda bi, i: (0, 0)
    return pl.pallas_call(
        functools.partial(_outproj_kernel, tm=tm, nt=nt),
        grid=(b, nt),
        in_specs=[
            pl.BlockSpec(memory_space=pl.ANY),
            pl.BlockSpec((1, tm, D_ATTN), row_blk),
            pl.BlockSpec((1, tm, D_CONV), row_blk),
            pl.BlockSpec(w_out.shape, const),
            pl.BlockSpec((1, d), const),
        ],
        out_specs=pl.BlockSpec((1, tm, d), row_blk),
        out_shape=jax.ShapeDtypeStruct((b, l, d), _F32),
        scratch_shapes=[pltpu.VMEM((X_SLOTS, tm, d), _F32),
                        pltpu.SemaphoreType.DMA((X_SLOTS,))],
        compiler_params=pltpu.CompilerParams(
            dimension_semantics=("arbitrary", "arbitrary"),
            vmem_limit_bytes=VMEM_LIMIT),
        name="outproj",
    )(x, ya, yc, w_out, g)


def kernel(x, meta, norm_g, w_in, b_f, conv_w, attn_norm_g, conv_norm_g, w_out, final_norm_g):
    assert norm_g.shape[0] == 1, "single layer only"
    b, seq, d = x.shape
    wt = w_in[0].T
    f0 = 3 * D_ATTN
    w_t = jnp.concatenate(
        [wt[:D_ATTN], wt[2 * D_ATTN:f0 + ATTN_HEADS], jnp.zeros((ATTN_HEADS, d), wt.dtype),
         wt[D_ATTN:2 * D_ATTN], wt[f0 + ATTN_HEADS:]], axis=0).astype(_BF16)
    bf = b_f[0][:, None]
    cw = conv_w[0]

    meta_rows = LANES
    meta_p = jnp.pad(meta.astype(x.dtype), ((0, meta_rows - N_META), (0, 0)))[None]
    zero_halo = jnp.zeros((SUBLANES, D_CONV), _F32)
    zero_carry = jnp.zeros((ATTN_HEADS, 1), _F32)
    _, kxm, vmt, _, _, cxm, crow = _inproj(
        meta_p, norm_g, w_t, bf, cw, conv_norm_g, zero_halo, zero_carry,
        tm=meta_rows, emit_tail=True)
    kxm = kxm[0, :N_META]
    vmt = vmt[0, :, :N_META]
    halo0 = cxm[0, N_META - SUBLANES:N_META]
    carry0 = crow[0, :, N_META - 1:N_META]

    qt, kx, vt, z, yc = _inproj(x, norm_g, w_t, bf, cw, conv_norm_g,
                                halo0, carry0, tm=INPROJ_ROWS, emit_tail=False)
    ya = _attention(qt, kx, vt, kxm, vmt, z, attn_norm_g, tq=ATTN_BLOCK)
    return _outproj(x, ya, yc, w_out[0].astype(_BF16), final_norm_g[None, :],
                    tm=OUTPROJ_ROWS)
```

```python
import functools
import math

import jax
import jax.numpy as jnp
from jax import lax
from jax.experimental import pallas as pl
from jax.experimental.pallas import tpu as pltpu

D_MODEL = 1024
N_META = 16
ATTN_HEADS = 8
HEAD_DIM = 64
D_ATTN = 512
D_CONV = 512
CONV_WIDTH = 3
EPS = 1e-6
LANES = 128
SUBLANES = 8
BF16_ROWS = 16
N_PIECES = 3
QVF_ROWS = 2 * D_ATTN + BF16_ROWS
ONES_ROWS = BF16_ROWS
LOG2E = math.log2(math.e)
VMEM_LIMIT = 48 * 1024 * 1024
INPROJ_ROWS = 1024
ATTN_BLOCK = 512
OUTPROJ_ROWS = 1024

_BF16 = jnp.bfloat16
_F32 = jnp.float32


def _log_sigmoid(x):
    return jnp.minimum(x, 0.0) - jnp.log1p(jnp.exp(-jnp.abs(x)))


def _silu(x):
    return x * (1.0 / (1.0 + jnp.exp(-x)))


def _dot_nt(a, b):
    return lax.dot_general(a, b, (((1,), (1,)), ((), ())),
                           preferred_element_type=_F32)


def _bf16_pieces(x):
    hi = x.astype(_BF16).astype(_F32)
    r1 = x - hi
    mid = r1.astype(_BF16).astype(_F32)
    return hi, mid, r1 - mid


def _inproj_kernel(x_ref, g_ref, w_ref, bf_ref, cw_ref, cg_ref, halo0_ref,
                   carry0_ref, qt_ref, kx_ref, vt_ref, z_ref, yc_ref, *rest,
                   tm, emit_tail):
    if emit_tail:
        cx_ref, crow_ref, carry_s, cx_s = rest
    else:
        carry_s, cx_s = rest
    i = pl.program_id(1)

    @pl.when(i == 0)
    def _():
        for d in (1, 2):
            cx_s[d - 1, SUBLANES:SUBLANES + d, :] = halo0_ref[SUBLANES - d:SUBLANES, :]
        carry_s[...] = jnp.broadcast_to(carry0_ref[...], (ATTN_HEADS, LANES))

    @pl.when(i > 0)
    def _():
        for d in (1, 2):
            cx_s[d - 1, SUBLANES:SUBLANES + d, :] = cx_s[d - 1, SUBLANES + tm:SUBLANES + tm + d, :]

    x = x_ref[0]
    ms = jnp.mean(x * x, axis=-1, keepdims=True)
    u = (x * lax.rsqrt(ms + EPS) * g_ref[...]).astype(_BF16)

    def proj(c):
        lo = QVF_ROWS + c * 512
        return _dot_nt(u, w_ref[lo:lo + 512, :])

    gate_b = proj(2)
    cx = proj(3) * proj(4)
    for d in (1, 2):
        cx_s[d - 1, SUBLANES + d:SUBLANES + d + tm, :] = cx
    if emit_tail:
        cx_ref[0] = cx
    conv = (cw_ref[0:1, :] * cx_s[1, SUBLANES:SUBLANES + tm, :]
            + cw_ref[1:2, :] * cx_s[0, SUBLANES:SUBLANES + tm, :]
            + cw_ref[2:3, :] * cx)
    yb = gate_b * conv
    zc = proj(5)
    low = lax.broadcasted_iota(jnp.int32, (1, LANES), 1) < HEAD_DIM
    inv_rms = []
    for cb in range(D_CONV // LANES):
        y = yb[:, cb * LANES:(cb + 1) * LANES]
        y2 = y * y
        s_lo = jnp.sum(jnp.where(low, y2, 0.0), axis=-1, keepdims=True)
        s_hi = jnp.sum(jnp.where(low, 0.0, y2), axis=-1, keepdims=True)
        msq = jnp.where(low, s_lo, s_hi) * (1.0 / HEAD_DIM)
        inv_rms.append(lax.rsqrt(msq + EPS))
    yn = yb * jnp.concatenate(inv_rms, axis=1) * cg_ref[...]
    yc_ref[0] = (yn * _silu(zc)).astype(_BF16)

    qvt = _dot_nt(w_ref[0:QVF_ROWS, :], u)
    qt_ref[0] = (qvt[:D_ATTN] * (LOG2E * HEAD_DIM ** -0.5)).astype(_BF16)
    vt_ref[0] = qvt[D_ATTN:2 * D_ATTN].astype(_BF16)

    logf = _log_sigmoid(qvt[2 * D_ATTN:2 * D_ATTN + ATTN_HEADS] + bf_ref[...])
    hi, mid, lo = _bf16_pieces(logf)
    pieces = jnp.concatenate([hi, mid, lo, jnp.zeros_like(hi)], axis=0).astype(_BF16)
    row = lax.broadcasted_iota(jnp.int32, (tm, tm), 0)
    col = lax.broadcasted_iota(jnp.int32, (tm, tm), 1)
    tri = jnp.where(row <= col, 1.0, 0.0).astype(_BF16)
    c3 = jnp.dot(pieces, tri, preferred_element_type=_F32)
    h8 = ATTN_HEADS
    csum = c3[0:h8] + c3[h8:2 * h8] + c3[2 * h8:3 * h8] + carry_s[:, 0:1]
    carry_s[...] = jnp.broadcast_to(csum[:, tm - 1:tm], (h8, LANES))
    if emit_tail:
        crow_ref[0] = csum
    hi, mid, lo = _bf16_pieces(csum * (-LOG2E))
    aug = jnp.concatenate(
        [jnp.zeros((HEAD_DIM, tm), _F32), hi, mid, lo,
         jnp.zeros((LANES - HEAD_DIM - N_PIECES * h8, tm), _F32)], axis=0).T

    k = proj(0)
    aug = aug.astype(_BF16)
    for p in range(D_ATTN // LANES):
        kp = k[:, p * LANES:(p + 1) * LANES]
        sl = slice(2 * p * LANES, (2 * p + 1) * LANES)
        kx_ref[0, :, sl] = jnp.where(low, kp.astype(_BF16), aug)
        sl = slice((2 * p + 1) * LANES, (2 * p + 2) * LANES)
        kx_ref[0, :, sl] = jnp.where(low, pltpu.roll(kp, HEAD_DIM, axis=1).astype(_BF16), aug)
    z_ref[0] = proj(1).astype(_BF16)


def _inproj(x3, g, w_t, bf, cw, cg, halo0, carry0, *, tm, emit_tail):
    b, l, d = x3.shape
    nt = l // tm
    kern = functools.partial(_inproj_kernel, tm=tm, emit_tail=emit_tail)
    const = lambda bi, i: (0, 0)
    row_blk = lambda bi, i: (bi, i, 0)
    col_blk = lambda bi, i: (bi, 0, i)
    rows = (jax.ShapeDtypeStruct((b, l, 512), _BF16), pl.BlockSpec((1, tm, 512), row_blk))
    cols = (jax.ShapeDtypeStruct((b, 512, l), _BF16), pl.BlockSpec((1, 512, tm), col_blk))
    keys = (jax.ShapeDtypeStruct((b, l, ATTN_HEADS * LANES), _BF16),
            pl.BlockSpec((1, tm, ATTN_HEADS * LANES), row_blk))
    out_shape, out_specs = (list(t) for t in zip(cols, keys, cols, rows, rows))
    if emit_tail:
        out_shape.append(jax.ShapeDtypeStruct((b, l, D_CONV), _F32))
        out_specs.append(pl.BlockSpec((1, tm, D_CONV), row_blk))
        out_shape.append(jax.ShapeDtypeStruct((b, ATTN_HEADS, l), _F32))
        out_specs.append(pl.BlockSpec((1, ATTN_HEADS, tm), col_blk))
    return pl.pallas_call(
        kern,
        grid=(b, nt),
        in_specs=[
            pl.BlockSpec((1, tm, d), row_blk),
            pl.BlockSpec((1, d), const),
            pl.BlockSpec(w_t.shape, const),
            pl.BlockSpec((ATTN_HEADS, 1), const),
            pl.BlockSpec((CONV_WIDTH, D_CONV), const),
            pl.BlockSpec((1, D_CONV), const),
            pl.BlockSpec((SUBLANES, D_CONV), const),
            pl.BlockSpec((ATTN_HEADS, 1), const),
        ],
        out_specs=out_specs,
        out_shape=out_shape,
        scratch_shapes=[
            pltpu.VMEM((ATTN_HEADS, LANES), _F32),
            pltpu.VMEM((2, tm + 2 * SUBLANES, D_CONV), _F32),
        ],
        compiler_params=pltpu.CompilerParams(
            dimension_semantics=("arbitrary", "arbitrary"),
            vmem_limit_bytes=VMEM_LIMIT),
        name="inproj_meta" if emit_tail else "inproj",
    )(x3, g, w_t, bf, cw, cg, halo0, carry0)


def _attn_kernel(qt_ref, kx_ref, vt_ref, kxm_ref, vmt_ref,
                 z_ref, g_ref, o_ref, q2t_s, sd1_buf, sd2_buf, su_buf, mbd1_s, mbd2_s, mbu_s,
                 mask1_s, mask2_s, m_s, acc_s, *, tq, n_q, pairs):
    n_pairs = len(pairs)
    tk = tq
    th = tq // 2
    pair = pl.program_id(1)
    ones_v = jnp.ones((ONES_ROWS, tk), _BF16)
    ones_m = jnp.ones((ONES_ROWS, N_META), _BF16)

    @pl.when(jnp.logical_and(pl.program_id(0) == 0, pair == 0))
    def _():
        c = lax.broadcasted_iota(jnp.int32, (th, 2 * tq), 1)
        q_loc = jnp.where(c >= tq, c - tq, c)
        k_loc = lax.broadcasted_iota(jnp.int32, (th, 2 * tq), 0)
        mask1_s[0:th, :] = jnp.where(k_loc <= q_loc, 0.0, -jnp.inf)
        mask1_s[th:, :] = jnp.zeros((N_META, 2 * tq), _F32)
        c = lax.broadcasted_iota(jnp.int32, (th, 2 * th), 1)
        q_loc = jnp.where(c >= th, c - th, c)
        k_loc = lax.broadcasted_iota(jnp.int32, (th, 2 * th), 0)
        mask2_s[...] = jnp.where(k_loc <= q_loc, 0.0, -jnp.inf)

    def ones_for(h):
        r = lax.broadcasted_iota(jnp.int32, (HEAD_DIM, 1), 0)
        sel = jnp.logical_and(r % ATTN_HEADS == h, r < N_PIECES * ATTN_HEADS)
        return jnp.broadcast_to(jnp.where(sel, 1.0, 0.0).astype(_BF16), (HEAD_DIM, tq))

    def build_q(qi, carry):
        qt = qt_ref[0, :, pl.ds(pl.multiple_of(qi * tq, tq), tq)]
        q2t_s[qi, 0:HEAD_DIM, 0:tq] = qt[0:HEAD_DIM]
        q2t_s[qi, 0:HEAD_DIM, tq:] = qt[HEAD_DIM:]
        q2t_s[qi, HEAD_DIM:, 0:tq] = ones_for(2 * pair)
        q2t_s[qi, HEAD_DIM:, tq:] = ones_for(2 * pair + 1)
        return carry

    lax.fori_loop(0, n_q, build_q, 0)

    def keys(j):
        return kx_ref[0, j * tk:(j + 1) * tk, :]

    def logits(kx, qa, qb):
        return jnp.concatenate(
            [jnp.dot(kx[:, :LANES], qa, preferred_element_type=_F32),
             jnp.dot(kx[:, LANES:], qb, preferred_element_type=_F32)], axis=1)

    def values(j):
        return jnp.concatenate([vt_ref[0, :, j * tk:(j + 1) * tk], ones_v], axis=0)

    def late(x):
        return jnp.concatenate([x[..., th:tq], x[..., tq + th:]], axis=-1)

    def produce_d1(qi, slot):
        kx = jnp.concatenate([kx_ref[0, qi * tk:qi * tk + th, :], kxm_ref[...]], axis=0)
        q2t = q2t_s[qi]
        s = logits(kx, q2t[:, :tq], q2t[:, tq:]) + mask1_s[...]
        sd1_buf[slot] = s
        mbd1_s[slot] = jnp.max(s, axis=0, keepdims=True)

    def consume_d1(qi, slot):
        m = mbd1_s[slot]
        p = jnp.exp2(sd1_buf[slot] - m).astype(_BF16)
        m_s[qi] = m
        vals = jnp.concatenate([vt_ref[0, :, qi * tk:qi * tk + th], ones_v[:, :th]], axis=0)
        acc_s[qi] = (jnp.dot(vals, p[0:th], preferred_element_type=_F32)
                     + jnp.dot(jnp.concatenate([vmt_ref[...], ones_m], axis=0), p[th:],
                               preferred_element_type=_F32))

    def produce_d2(qi, slot):
        kx = kx_ref[0, qi * tk + th:(qi + 1) * tk, :]
        q2t = q2t_s[qi]
        s = logits(kx, q2t[:, th:tq], q2t[:, tq + th:]) + mask2_s[...]
        sd2_buf[slot] = s
        mbd2_s[slot] = jnp.max(s, axis=0, keepdims=True)

    def consume_d2(qi, slot):
        m = late(m_s[qi])
        m_new = jnp.maximum(m, mbd2_s[slot])
        alpha = jnp.exp2(m - m_new)
        p = jnp.exp2(sd2_buf[slot] - m_new).astype(_BF16)
        vals = jnp.concatenate([vt_ref[0, :, qi * tk + th:(qi + 1) * tk], ones_v[:, :th]],
                               axis=0)
        acc = alpha * late(acc_s[qi]) + jnp.dot(vals, p, preferred_element_type=_F32)
        m_s[qi, :, th:tq] = m_new[:, :th]
        m_s[qi, :, tq + th:] = m_new[:, th:]
        acc_s[qi, :, th:tq] = acc[:, :th]
        acc_s[qi, :, tq + th:] = acc[:, th:]

    def produce(t, slot):
        qi, j = pairs[t]
        q2t = q2t_s[qi]
        s = logits(keys(j), q2t[:, :tq], q2t[:, tq:])
        su_buf[slot] = s
        mbu_s[slot] = jnp.max(s, axis=0, keepdims=True)

    def consume(t, slot):
        qi, j = pairs[t]
        m = m_s[qi]
        m_new = jnp.maximum(m, mbu_s[slot])
        alpha = jnp.exp2(m - m_new)
        p = jnp.exp2(su_buf[slot] - m_new).astype(_BF16)
        m_s[qi] = m_new
        acc_s[qi] = alpha * acc_s[qi] + jnp.dot(values(j), p, preferred_element_type=_F32)

    def finalize(qi):
        rows = slice(qi * tq, (qi + 1) * tq)
        inv_l = 1.0 / acc_s[qi, LANES:LANES + 1, :]

        def head_norm(rows_h, cols_h):
            o = acc_s[qi, rows_h, cols_h] * inv_l[:, cols_h]
            msq = jnp.sum(o * o, axis=0, keepdims=True) * (1.0 / HEAD_DIM)
            return o * lax.rsqrt(msq + EPS)

        on = jnp.concatenate([head_norm(slice(0, HEAD_DIM), slice(0, tq)),
                              head_norm(slice(HEAD_DIM, LANES), slice(tq, 2 * tq))], axis=0)
        y = on.T * g_ref[...]
        z = z_ref[0, rows, :].astype(_F32)
        o_ref[0, rows, :] = (y * _silu(z)).astype(_BF16)

    def diag_blocks(q_lo, q_hi):
        for qi in range(q_lo, q_hi):
            produce_d2(qi, qi % 2)
            consume_d1(qi, qi % 2)
            if qi + 1 < n_q:
                produce_d1(qi + 1, (qi + 1) % 2)
            else:
                produce(0, 0)
            consume_d2(qi, qi % 2)
            if qi == 0:
                finalize(0)

    produce_d1(0, 0)
    diag_blocks(0, n_q // 2)
    pl.when(pl.program_id(1) >= 0)(functools.partial(diag_blocks, n_q // 2, n_q))

    @pl.when(pl.program_id(0) >= 0)
    def _():
        for t in range(1, n_pairs + 1):
            if t < n_pairs:
                produce(t, t % 2)
            consume(t - 1, (t - 1) % 2)
            qi, j = pairs[t - 1]
            if j == qi - 1:
                finalize(qi)


def _attention(qt, kx, vt, kxm, vmt, z, g, *, tq):
    b, l, _ = z.shape
    n_pair = D_ATTN // LANES
    n_q = l // tq
    assert n_q >= 2 and l % tq == 0
    pairs = tuple((qi, j) for qi in range(n_q) for j in range(qi))
    kern = functools.partial(_attn_kernel, tq=tq, n_q=n_q, pairs=pairs)
    seq_rows = pl.BlockSpec((1, l, LANES), lambda bi, p: (bi, 0, p))
    seq_cols = pl.BlockSpec((1, LANES, l), lambda bi, p: (bi, p, 0))
    th = tq // 2
    return pl.pallas_call(
        kern,
        grid=(b, n_pair),
        in_specs=[
            seq_cols,
            pl.BlockSpec((1, l, 2 * LANES), lambda bi, p: (bi, 0, p)),
            seq_cols,
            pl.BlockSpec((N_META, 2 * LANES), lambda bi, p: (0, p)),
            pl.BlockSpec((LANES, N_META), lambda bi, p: (p, 0)),
            seq_rows,
            pl.BlockSpec((1, LANES), lambda bi, p: (0, p)),
        ],
        out_specs=seq_rows,
        scratch_shapes=[
            pltpu.VMEM((n_q, LANES, 2 * tq), _BF16),
            pltpu.VMEM((2, th + N_META, 2 * tq), _F32),
            pltpu.VMEM((2, th, 2 * th), _F32),
            pltpu.VMEM((2, tq, 2 * tq), _F32),
            pltpu.VMEM((2, 1, 2 * tq), _F32),
            pltpu.VMEM((2, 1, 2 * th), _F32),
            pltpu.VMEM((2, 1, 2 * tq), _F32),
            pltpu.VMEM((th + N_META, 2 * tq), _F32),
            pltpu.VMEM((th, 2 * th), _F32),
            pltpu.VMEM((n_q, 1, 2 * tq), _F32),
            pltpu.VMEM((n_q, LANES + ONES_ROWS, 2 * tq), _F32),
        ],
        out_shape=jax.ShapeDtypeStruct((b, l, D_ATTN), _BF16),
        compiler_params=pltpu.CompilerParams(
            dimension_semantics=("arbitrary", "arbitrary"),
            vmem_limit_bytes=VMEM_LIMIT),
        name="fox_attention",
    )(qt, kx, vt, kxm, vmt, z, g)


X_SLOTS = 3


def _outproj_kernel(x_hbm, ya_ref, yc_ref, w_ref, g_ref, o_ref, x_buf, x_sem, *, tm, nt):
    step = pl.program_id(0) * nt + pl.program_id(1)
    n_steps = pl.num_programs(0) * nt

    def x_copy(s):
        slot = s % X_SLOTS
        rows = pl.ds(pl.multiple_of((s % nt) * tm, tm), tm)
        return pltpu.make_async_copy(x_hbm.at[s // nt, rows, :], x_buf.at[slot], x_sem.at[slot])

    @pl.when(step == 0)
    def _():
        for s in range(X_SLOTS - 1):
            x_copy(s).start()

    @pl.when(step + X_SLOTS - 1 < n_steps)
    def _():
        x_copy(step + X_SLOTS - 1).start()

    x_copy(step).wait()
    hres = (x_buf[step % X_SLOTS]
            + jnp.dot(ya_ref[0], w_ref[0:D_ATTN, :], preferred_element_type=_F32)
            + jnp.dot(yc_ref[0], w_ref[D_ATTN:, :], preferred_element_type=_F32))
    ms = jnp.mean(hres * hres, axis=-1, keepdims=True)
    o_ref[0] = hres * lax.rsqrt(ms + EPS) * g_ref[...]


def _outproj(x, ya, yc, w_out, g, *, tm):
    b, l, d = x.shape
    nt = l // tm
    assert b * nt >= X_SLOTS - 1
    row_blk = lambda bi, i: (bi, i, 0)
    const = lambda bi, i: (0, 0)
    return pl.pallas_call(
        functools.partial(_outproj_kernel, tm=tm, nt=nt),
        grid=(b, nt),
        in_specs=[
            pl.BlockSpec(memory_space=pl.ANY),
            pl.BlockSpec((1, tm, D_ATTN), row_blk),
            pl.BlockSpec((1, tm, D_CONV), row_blk),
            pl.BlockSpec(w_out.shape, const),
            pl.BlockSpec((1, d), const),
        ],
        out_specs=pl.BlockSpec((1, tm, d), row_blk),
        out_shape=jax.ShapeDtypeStruct((b, l, d), _F32),
        scratch_shapes=[pltpu.VMEM((X_SLOTS, tm, d), _F32),
                        pltpu.SemaphoreType.DMA((X_SLOTS,))],
        compiler_params=pltpu.CompilerParams(
            dimension_semantics=("arbitrary", "arbitrary"),
            vmem_limit_bytes=VMEM_LIMIT),
        name="outproj",
    )(x, ya, yc, w_out, g)


def kernel(x, meta, norm_g, w_in, b_f, conv_w, attn_norm_g, conv_norm_g, w_out, final_norm_g):
    assert norm_g.shape[0] == 1, "single layer only"
    b, seq, d = x.shape
    wt = w_in[0].T
    f0 = 3 * D_ATTN
    w_t = jnp.concatenate(
        [wt[:D_ATTN], wt[2 * D_ATTN:f0 + ATTN_HEADS], jnp.zeros((ATTN_HEADS, d), wt.dtype),
         wt[D_ATTN:2 * D_ATTN], wt[f0 + ATTN_HEADS:]], axis=0).astype(_BF16)
    bf = b_f[0][:, None]
    cw = conv_w[0]

    meta_rows = LANES
    meta_p = jnp.pad(meta.astype(x.dtype), ((0, meta_rows - N_META), (0, 0)))[None]
    zero_halo = jnp.zeros((SUBLANES, D_CONV), _F32)
    zero_carry = jnp.zeros((ATTN_HEADS, 1), _F32)
    _, kxm, vmt, _, _, cxm, crow = _inproj(
        meta_p, norm_g, w_t, bf, cw, conv_norm_g, zero_halo, zero_carry,
        tm=meta_rows, emit_tail=True)
    kxm = kxm[0, :N_META]
    vmt = vmt[0, :, :N_META]
    halo0 = cxm[0, N_META - SUBLANES:N_META]
    carry0 = crow[0, :, N_META - 1:N_META]

    qt, kx, vt, z, yc = _inproj(x, norm_g, w_t, bf, cw, conv_norm_g,
                                halo0, carry0, tm=INPROJ_ROWS, emit_tail=False)
    ya = _attention(qt, kx, vt, kxm, vmt, z, attn_norm_g, tq=ATTN_BLOCK)
    return _outproj(x, ya, yc, w_out[0].astype(_BF16), final_norm_g[None, :],
                    tm=OUTPROJ_ROWS)
```

```python
import functools
import math

import jax
import jax.numpy as jnp
from jax import lax
from jax.experimental import pallas as pl
from jax.experimental.pallas import tpu as pltpu

D_MODEL = 1024
N_META = 16
ATTN_HEADS = 8
HEAD_DIM = 64
D_ATTN = 512
D_CONV = 512
CONV_WIDTH = 3
EPS = 1e-6
LANES = 128
SUBLANES = 8
BF16_ROWS = 16
N_PIECES = 3
QVF_ROWS = 2 * D_ATTN + BF16_ROWS
ONES_ROWS = BF16_ROWS
LOG2E = math.log2(math.e)
VMEM_LIMIT = 48 * 1024 * 1024
INPROJ_ROWS = 1024
ATTN_BLOCK = 512
OUTPROJ_ROWS = 1024

_BF16 = jnp.bfloat16
_F32 = jnp.float32


def _log_sigmoid(x):
    return jnp.minimum(x, 0.0) - jnp.log1p(jnp.exp(-jnp.abs(x)))


def _silu(x):
    return x * (1.0 / (1.0 + jnp.exp(-x)))


def _dot_nt(a, b):
    return lax.dot_general(a, b, (((1,), (1,)), ((), ())),
                           preferred_element_type=_F32)


def _bf16_pieces(x):
    hi = x.astype(_BF16).astype(_F32)
    r1 = x - hi
    mid = r1.astype(_BF16).astype(_F32)
    return hi, mid, r1 - mid


def _inproj_kernel(x_ref, g_ref, w_ref, bf_ref, cw_ref, cg_ref, halo0_ref,
                   carry0_ref, qt_ref, kx_ref, vt_ref, z_ref, yc_ref, *rest,
                   tm, emit_tail):
    if emit_tail:
        cx_ref, crow_ref, carry_s, cx_s = rest
    else:
        carry_s, cx_s = rest
    i = pl.program_id(1)

    @pl.when(i == 0)
    def _():
        for d in (1, 2):
            cx_s[d - 1, SUBLANES:SUBLANES + d, :] = halo0_ref[SUBLANES - d:SUBLANES, :]
        carry_s[...] = jnp.broadcast_to(carry0_ref[...], (ATTN_HEADS, LANES))

    @pl.when(i > 0)
    def _():
        for d in (1, 2):
            cx_s[d - 1, SUBLANES:SUBLANES + d, :] = cx_s[d - 1, SUBLANES + tm:SUBLANES + tm + d, :]

    x = x_ref[0]
    ms = jnp.mean(x * x, axis=-1, keepdims=True)
    u = (x * lax.rsqrt(ms + EPS) * g_ref[...]).astype(_BF16)

    def proj(c):
        lo = QVF_ROWS + c * 512
        return _dot_nt(u, w_ref[lo:lo + 512, :])

    gate_b = proj(2)
    cx = proj(3) * proj(4)
    for d in (1, 2):
        cx_s[d - 1, SUBLANES + d:SUBLANES + d + tm, :] = cx
    if emit_tail:
        cx_ref[0] = cx
    conv = (cw_ref[0:1, :] * cx_s[1, SUBLANES:SUBLANES + tm, :]
            + cw_ref[1:2, :] * cx_s[0, SUBLANES:SUBLANES + tm, :]
            + cw_ref[2:3, :] * cx)
    yb = gate_b * conv
    zc = proj(5)
    low = lax.broadcasted_iota(jnp.int32, (1, LANES), 1) < HEAD_DIM
    inv_rms = []
    for cb in range(D_CONV // LANES):
        y = yb[:, cb * LANES:(cb + 1) * LANES]
        y2 = y * y
        s_lo = jnp.sum(jnp.where(low, y2, 0.0), axis=-1, keepdims=True)
        s_hi = jnp.sum(jnp.where(low, 0.0, y2), axis=-1, keepdims=True)
        msq = jnp.where(low, s_lo, s_hi) * (1.0 / HEAD_DIM)
        inv_rms.append(lax.rsqrt(msq + EPS))
    yn = yb * jnp.concatenate(inv_rms, axis=1) * cg_ref[...]
    yc_ref[0] = (yn * _silu(zc)).astype(_BF16)

    qvt = _dot_nt(w_ref[0:QVF_ROWS, :], u)
    qt_ref[0] = (qvt[:D_ATTN] * (LOG2E * HEAD_DIM ** -0.5)).astype(_BF16)
    vt_ref[0] = qvt[D_ATTN:2 * D_ATTN].astype(_BF16)

    logf = _log_sigmoid(qvt[2 * D_ATTN:2 * D_ATTN + ATTN_HEADS] + bf_ref[...])
    hi, mid, lo = _bf16_pieces(logf)
    pieces = jnp.concatenate([hi, mid, lo, jnp.zeros_like(hi)], axis=0).astype(_BF16)
    row = lax.broadcasted_iota(jnp.int32, (tm, tm), 0)
    col = lax.broadcasted_iota(jnp.int32, (tm, tm), 1)
    tri = jnp.where(row <= col, 1.0, 0.0).astype(_BF16)
    c3 = jnp.dot(pieces, tri, preferred_element_type=_F32)
    h8 = ATTN_HEADS
    csum = c3[0:h8] + c3[h8:2 * h8] + c3[2 * h8:3 * h8] + carry_s[:, 0:1]
    carry_s[...] = jnp.broadcast_to(csum[:, tm - 1:tm], (h8, LANES))
    if emit_tail:
        crow_ref[0] = csum
    hi, mid, lo = _bf16_pieces(csum * (-LOG2E))
    aug = jnp.concatenate(
        [jnp.zeros((HEAD_DIM, tm), _F32), hi, mid, lo,
         jnp.zeros((LANES - HEAD_DIM - N_PIECES * h8, tm), _F32)], axis=0).T

    k = proj(0)
    aug = aug.astype(_BF16)
    for p in range(D_ATTN // LANES):
        kp = k[:, p * LANES:(p + 1) * LANES]
        sl = slice(2 * p * LANES, (2 * p + 1) * LANES)
        kx_ref[0, :, sl] = jnp.where(low, kp.astype(_BF16), aug)
        sl = slice((2 * p + 1) * LANES, (2 * p + 2) * LANES)
        kx_ref[0, :, sl] = jnp.where(low, pltpu.roll(kp, HEAD_DIM, axis=1).astype(_BF16), aug)
    z_ref[0] = proj(1).astype(_BF16)


def _inproj(x3, g, w_t, bf, cw, cg, halo0, carry0, *, tm, emit_tail):
    b, l, d = x3.shape
    nt = l // tm
    kern = functools.partial(_inproj_kernel, tm=tm, emit_tail=emit_tail)
    const = lambda bi, i: (0, 0)
    row_blk = lambda bi, i: (bi, i, 0)
    col_blk = lambda bi, i: (bi, 0, i)
    rows = (jax.ShapeDtypeStruct((b, l, 512), _BF16), pl.BlockSpec((1, tm, 512), row_blk))
    cols = (jax.ShapeDtypeStruct((b, 512, l), _BF16), pl.BlockSpec((1, 512, tm), col_blk))
    keys = (jax.ShapeDtypeStruct((b, l, ATTN_HEADS * LANES), _BF16),
            pl.BlockSpec((1, tm, ATTN_HEADS * LANES), row_blk))
    out_shape, out_specs = (list(t) for t in zip(cols, keys, cols, rows, rows))
    if emit_tail:
        out_shape.append(jax.ShapeDtypeStruct((b, l, D_CONV), _F32))
        out_specs.append(pl.BlockSpec((1, tm, D_CONV), row_blk))
        out_shape.append(jax.ShapeDtypeStruct((b, ATTN_HEADS, l), _F32))
        out_specs.append(pl.BlockSpec((1, ATTN_HEADS, tm), col_blk))
    return pl.pallas_call(
        kern,
        grid=(b, nt),
        in_specs=[
            pl.BlockSpec((1, tm, d), row_blk),
            pl.BlockSpec((1, d), const),
            pl.BlockSpec(w_t.shape, const),
            pl.BlockSpec((ATTN_HEADS, 1), const),
            pl.BlockSpec((CONV_WIDTH, D_CONV), const),
            pl.BlockSpec((1, D_CONV), const),
            pl.BlockSpec((SUBLANES, D_CONV), const),
            pl.BlockSpec((ATTN_HEADS, 1), const),
        ],
        out_specs=out_specs,
        out_shape=out_shape,
        scratch_shapes=[
            pltpu.VMEM((ATTN_HEADS, LANES), _F32),
            pltpu.VMEM((2, tm + 2 * SUBLANES, D_CONV), _F32),
        ],
        compiler_params=pltpu.CompilerParams(
            dimension_semantics=("arbitrary", "arbitrary"),
            vmem_limit_bytes=VMEM_LIMIT),
        name="inproj_meta" if emit_tail else "inproj",
    )(x3, g, w_t, bf, cw, cg, halo0, carry0)


def _attn_kernel(qt_ref, kx_ref, vt_ref, kxm_ref, vmt_ref,
                 z_ref, g_ref, o_ref, q2t_s, sd1_buf, sd2_buf, su_buf, mbd1_s, mbd2_s, mbu_s,
                 mask1_s, mask2_s, m_s, acc_s, *, tq, n_q, pairs):
    n_pairs = len(pairs)
    tk = tq
    th = tq // 2
    pair = pl.program_id(1)
    ones_v = jnp.ones((ONES_ROWS, tk), _BF16)
    ones_m = jnp.ones((ONES_ROWS, N_META), _BF16)

    @pl.when(jnp.logical_and(pl.program_id(0) == 0, pair == 0))
    def _():
        c = lax.broadcasted_iota(jnp.int32, (th, 2 * tq), 1)
        q_loc = jnp.where(c >= tq, c - tq, c)
        k_loc = lax.broadcasted_iota(jnp.int32, (th, 2 * tq), 0)
        mask1_s[0:th, :] = jnp.where(k_loc <= q_loc, 0.0, -jnp.inf)
        mask1_s[th:, :] = jnp.zeros((N_META, 2 * tq), _F32)
        c = lax.broadcasted_iota(jnp.int32, (th, 2 * th), 1)
        q_loc = jnp.where(c >= th, c - th, c)
        k_loc = lax.broadcasted_iota(jnp.int32, (th, 2 * th), 0)
        mask2_s[...] = jnp.where(k_loc <= q_loc, 0.0, -jnp.inf)

    def ones_for(h):
        r = lax.broadcasted_iota(jnp.int32, (HEAD_DIM, 1), 0)
        sel = jnp.logical_and(r % ATTN_HEADS == h, r < N_PIECES * ATTN_HEADS)
        return jnp.broadcast_to(jnp.where(sel, 1.0, 0.0).astype(_BF16), (HEAD_DIM, tq))

    def build_q(qi, carry):
        qt = qt_ref[0, :, pl.ds(pl.multiple_of(qi * tq, tq), tq)]
        q2t_s[qi, 0:HEAD_DIM, 0:tq] = qt[0:HEAD_DIM]
        q2t_s[qi, 0:HEAD_DIM, tq:] = qt[HEAD_DIM:]
        q2t_s[qi, HEAD_DIM:, 0:tq] = ones_for(2 * pair)
        q2t_s[qi, HEAD_DIM:, tq:] = ones_for(2 * pair + 1)
        return carry

    lax.fori_loop(0, n_q, build_q, 0)

    def keys(j):
        return kx_ref[0, j * tk:(j + 1) * tk, :]

    def logits(kx, qa, qb):
        return jnp.concatenate(
            [jnp.dot(kx[:, :LANES], qa, preferred_element_type=_F32),
             jnp.dot(kx[:, LANES:], qb, preferred_element_type=_F32)], axis=1)

    def values(j):
        return jnp.concatenate([vt_ref[0, :, j * tk:(j + 1) * tk], ones_v], axis=0)

    def late(x):
        return jnp.concatenate([x[..., th:tq], x[..., tq + th:]], axis=-1)

    def produce_d1(qi, slot):
        kx = jnp.concatenate([kx_ref[0, qi * tk:qi * tk + th, :], kxm_ref[...]], axis=0)
        q2t = q2t_s[qi]
        s = logits(kx, q2t[:, :tq], q2t[:, tq:]) + mask1_s[...]
        sd1_buf[slot] = s
        mbd1_s[slot] = jnp.max(s, axis=0, keepdims=True)

    def consume_d1(qi, slot):
        m = mbd1_s[slot]
        p = jnp.exp2(sd1_buf[slot] - m).astype(_BF16)
        m_s[qi] = m
        vals = jnp.concatenate([vt_ref[0, :, qi * tk:qi * tk + th], ones_v[:, :th]], axis=0)
        acc_s[qi] = (jnp.dot(vals, p[0:th], preferred_element_type=_F32)
                     + jnp.dot(jnp.concatenate([vmt_ref[...], ones_m], axis=0), p[th:],
                               preferred_element_type=_F32))

    def produce_d2(qi, slot):
        kx = kx_ref[0, qi * tk + th:(qi + 1) * tk, :]
        q2t = q2t_s[qi]
        s = logits(kx, q2t[:, th:tq], q2t[:, tq + th:]) + mask2_s[...]
        sd2_buf[slot] = s
        mbd2_s[slot] = jnp.max(s, axis=0, keepdims=True)

    def consume_d2(qi, slot):
        m = late(m_s[qi])
        m_new = jnp.maximum(m, mbd2_s[slot])
        alpha = jnp.exp2(m - m_new)
        p = jnp.exp2(sd2_buf[slot] - m_new).astype(_BF16)
        vals = jnp.concatenate([vt_ref[0, :, qi * tk + th:(qi + 1) * tk], ones_v[:, :th]],
                               axis=0)
        acc = alpha * late(acc_s[qi]) + jnp.dot(vals, p, preferred_element_type=_F32)
        m_s[qi, :, th:tq] = m_new[:, :th]
        m_s[qi, :, tq + th:] = m_new[:, th:]
        acc_s[qi, :, th:tq] = acc[:, :th]
        acc_s[qi, :, tq + th:] = acc[:, th:]

    def produce(t, slot):
        qi, j = pairs[t]
        q2t = q2t_s[qi]
        s = logits(keys(j), q2t[:, :tq], q2t[:, tq:])
        su_buf[slot] = s
        mbu_s[slot] = jnp.max(s, axis=0, keepdims=True)

    def consume(t, slot):
        qi, j = pairs[t]
        m = m_s[qi]
        m_new = jnp.maximum(m, mbu_s[slot])
        alpha = jnp.exp2(m - m_new)
        p = jnp.exp2(su_buf[slot] - m_new).astype(_BF16)
        m_s[qi] = m_new
        acc_s[qi] = alpha * acc_s[qi] + jnp.dot(values(j), p, preferred_element_type=_F32)

    def finalize(qi):
        rows = slice(qi * tq, (qi + 1) * tq)
        inv_l = 1.0 / acc_s[qi, LANES:LANES + 1, :]

        def head_norm(rows_h, cols_h):
            o = acc_s[qi, rows_h, cols_h] * inv_l[:, cols_h]
            msq = jnp.sum(o * o, axis=0, keepdims=True) * (1.0 / HEAD_DIM)
            return o * lax.rsqrt(msq + EPS)

        on = jnp.concatenate([head_norm(slice(0, HEAD_DIM), slice(0, tq)),
                              head_norm(slice(HEAD_DIM, LANES), slice(tq, 2 * tq))], axis=0)
        y = on.T * g_ref[...]
        z = z_ref[0, rows, :].astype(_F32)
        o_ref[0, rows, :] = (y * _silu(z)).astype(_BF16)

    def diag_blocks(q_lo, q_hi):
        for qi in range(q_lo, q_hi):
            produce_d2(qi, qi % 2)
            consume_d1(qi, qi % 2)
            if qi + 1 < n_q:
                produce_d1(qi + 1, (qi + 1) % 2)
            else:
                produce(0, 0)
            consume_d2(qi, qi % 2)
            if qi == 0:
                finalize(0)

    produce_d1(0, 0)
    diag_blocks(0, n_q // 2)
    pl.when(pl.program_id(1) >= 0)(functools.partial(diag_blocks, n_q // 2, n_q))

    @pl.when(pl.program_id(0) >= 0)
    def _():
        for t in range(1, n_pairs + 1):
            if t < n_pairs:
                produce(t, t % 2)
            consume(t - 1, (t - 1) % 2)
            qi, j = pairs[t - 1]
            if j == qi - 1:
                finalize(qi)


def _attention(qt, kx, vt, kxm, vmt, z, g, *, tq):
    b, l, _ = z.shape
    n_pair = D_ATTN // LANES
    n_q = l // tq
    assert n_q >= 2 and l % tq == 0
    pairs = tuple((qi, j) for qi in range(n_q) for j in range(qi))
    kern = functools.partial(_attn_kernel, tq=tq, n_q=n_q, pairs=pairs)
    seq_rows = pl.BlockSpec((1, l, LANES), lambda bi, p: (bi, 0, p))
    seq_cols = pl.BlockSpec((1, LANES, l), lambda bi, p: (bi, p, 0))
    th = tq // 2
    return pl.pallas_call(
        kern,
        grid=(b, n_pair),
        in_specs=[
            seq_cols,
            pl.BlockSpec((1, l, 2 * LANES), lambda bi, p: (bi, 0, p)),
            seq_cols,
            pl.BlockSpec((N_META, 2 * LANES), lambda bi, p: (0, p)),
            pl.BlockSpec((LANES, N_META), lambda bi, p: (p, 0)),
            seq_rows,
            pl.BlockSpec((1, LANES), lambda bi, p: (0, p)),
        ],
        out_specs=seq_rows,
        scratch_shapes=[
            pltpu.VMEM((n_q, LANES, 2 * tq), _BF16),
            pltpu.VMEM((2, th + N_META, 2 * tq), _F32),
            pltpu.VMEM((2, th, 2 * th), _F32),
            pltpu.VMEM((2, tq, 2 * tq), _F32),
            pltpu.VMEM((2, 1, 2 * tq), _F32),
            pltpu.VMEM((2, 1, 2 * th), _F32),
            pltpu.VMEM((2, 1, 2 * tq), _F32),
            pltpu.VMEM((th + N_META, 2 * tq), _F32),
            pltpu.VMEM((th, 2 * th), _F32),
            pltpu.VMEM((n_q, 1, 2 * tq), _F32),
            pltpu.VMEM((n_q, LANES + ONES_ROWS, 2 * tq), _F32),
        ],
        out_shape=jax.ShapeDtypeStruct((b, l, D_ATTN), _BF16),
        compiler_params=pltpu.CompilerParams(
            dimension_semantics=("arbitrary", "arbitrary"),
            vmem_limit_bytes=VMEM_LIMIT),
        name="fox_attention",
    )(qt, kx, vt, kxm, vmt, z, g)


X_SLOTS = 3


def _outproj_step(x_ref, ya_ref, yc_ref, w_ref, g_ref, o_ref):
    hres = (x_ref[0]
            + jnp.dot(ya_ref[0], w_ref[0:D_ATTN, :], preferred_element_type=_F32)
            + jnp.dot(yc_ref[0], w_ref[D_ATTN:, :], preferred_element_type=_F32))
    ms = jnp.mean(hres * hres, axis=-1, keepdims=True)
    o_ref[0] = hres * lax.rsqrt(ms + EPS) * g_ref[...]


def _outproj_kernel(x_hbm, ya_hbm, yc_hbm, w_hbm, g_hbm, o_hbm, *, tm):
    b, l, d = x_hbm.shape
    row_blk = lambda bi, i: (bi, i, 0)
    const = lambda bi, i: (0, 0)

    def rows(width):
        return pl.BlockSpec((1, tm, width), row_blk, pipeline_mode=pl.Buffered(X_SLOTS))

    pltpu.emit_pipeline(
        _outproj_step,
        grid=(b, l // tm),
        in_specs=[rows(d), rows(D_ATTN), rows(D_CONV),
                  pl.BlockSpec(w_hbm.shape, const), pl.BlockSpec((1, d), const)],
        out_specs=[pl.BlockSpec((1, tm, d), row_blk)],
    )(x_hbm, ya_hbm, yc_hbm, w_hbm, g_hbm, o_hbm)


def _outproj(x, ya, yc, w_out, g, *, tm):
    b, l, d = x.shape
    assert l % tm == 0
    any_spec = pl.BlockSpec(memory_space=pl.ANY)
    return pl.pallas_call(
        functools.partial(_outproj_kernel, tm=tm),
        in_specs=[any_spec] * 5,
        out_specs=any_spec,
        out_shape=jax.ShapeDtypeStruct((b, l, d), _F32),
        compiler_params=pltpu.CompilerParams(vmem_limit_bytes=VMEM_LIMIT),
        name="outproj",
    )(x, ya, yc, w_out, g)


def kernel(x, meta, norm_g, w_in, b_f, conv_w, attn_norm_g, conv_norm_g, w_out, final_norm_g):
    assert norm_g.shape[0] == 1, "single layer only"
    b, seq, d = x.shape
    wt = w_in[0].T
    f0 = 3 * D_ATTN
    w_t = jnp.concatenate(
        [wt[:D_ATTN], wt[2 * D_ATTN:f0 + ATTN_HEADS], jnp.zeros((ATTN_HEADS, d), wt.dtype),
         wt[D_ATTN:2 * D_ATTN], wt[f0 + ATTN_HEADS:]], axis=0).astype(_BF16)
    bf = b_f[0][:, None]
    cw = conv_w[0]

    meta_rows = LANES
    meta_p = jnp.pad(meta.astype(x.dtype), ((0, meta_rows - N_META), (0, 0)))[None]
    zero_halo = jnp.zeros((SUBLANES, D_CONV), _F32)
    zero_carry = jnp.zeros((ATTN_HEADS, 1), _F32)
    _, kxm, vmt, _, _, cxm, crow = _inproj(
        meta_p, norm_g, w_t, bf, cw, conv_norm_g, zero_halo, zero_carry,
        tm=meta_rows, emit_tail=True)
    kxm = kxm[0, :N_META]
    vmt = vmt[0, :, :N_META]
    halo0 = cxm[0, N_META - SUBLANES:N_META]
    carry0 = crow[0, :, N_META - 1:N_META]

    qt, kx, vt, z, yc = _inproj(x, norm_g, w_t, bf, cw, conv_norm_g,
                                halo0, carry0, tm=INPROJ_ROWS, emit_tail=False)
    ya = _attention(qt, kx, vt, kxm, vmt, z, attn_norm_g, tq=ATTN_BLOCK)
    return _outproj(x, ya, yc, w_out[0].astype(_BF16), final_norm_g[None, :],
                    tm=OUTPROJ_ROWS)
```
